```python
import jax, jax.numpy as jnp
from jax import lax
import numpy as np

D_MODEL = 1024
BATCH = 2
SEQ = 16384
DEPTH = 2

HEAD_DIM = 64
A_CH = D_MODEL // 2
A_CONV = 31
N_Q_HEADS = (D_MODEL // 2) // HEAD_DIM
N_KV_HEADS = 2
GROUP = N_Q_HEADS // N_KV_HEADS
WINDOW = 128
BLOCK = 128
ROPE_THETA = 500000.0
ROPE_DIM = HEAD_DIM // 4
Q_DIM = N_Q_HEADS * HEAD_DIM
KV_DIM = N_KV_HEADS * HEAD_DIM
EVEN_IN = 2 * A_CH + Q_DIM + 2 * KV_DIM
MIX_DIM = A_CH + Q_DIM
SC_DIM = D_MODEL
SC_CONV = 3
D_FF = 2816
FFN_CONV = 3
N_EVEN = (DEPTH + 1) // 2
N_ODD = DEPTH // 2
RMS_EPS = 1e-6
LN_EPS = 1e-5

kernel_name = "hybrid_conformer_swa_shortconv_trunk"


def rms_norm(x, g):
    xf = x.astype(jnp.float32)
    y = xf * lax.rsqrt(jnp.mean(xf * xf, axis=-1, keepdims=True) + RMS_EPS)
    return (y * g.astype(jnp.float32)).astype(x.dtype)


def layer_norm(x, g, b):
    xf = x.astype(jnp.float32)
    mu = jnp.mean(xf, axis=-1, keepdims=True)
    xc = xf - mu
    y = xc * lax.rsqrt(jnp.mean(xc * xc, axis=-1, keepdims=True) + LN_EPS)
    return (y * g.astype(jnp.float32) + b.astype(jnp.float32)).astype(x.dtype)


def causal_dwconv(x, w):
    k, c = w.shape
    return lax.conv_general_dilated(
        x, w[:, None, :].astype(x.dtype), window_strides=(1,), padding=[(k - 1, 0)],
        dimension_numbers=('NWC', 'WIO', 'NWC'), feature_group_count=c)


def partial_rope(x, positions):
    half = ROPE_DIM // 2
    inv_freq = ROPE_THETA ** (-(jnp.arange(half, dtype=jnp.float32) * 2.0 / ROPE_DIM))
    ang = positions.astype(jnp.float32)[..., None] * inv_freq
    cos = jnp.cos(ang)[:, :, None, :]
    sin = jnp.sin(ang)[:, :, None, :]
    xf = x.astype(jnp.float32)
    x1, x2, rest = xf[..., :half], xf[..., half:ROPE_DIM], xf[..., ROPE_DIM:]
    out = jnp.concatenate([x1 * cos - x2 * sin, x2 * cos + x1 * sin, rest], axis=-1)
    return out.astype(x.dtype)


def sliding_window_attention(q, k, v, sinks):
    bsz, s_len = q.shape[0], q.shape[1]
    nb = s_len // BLOCK
    qb = q.reshape(bsz, nb, BLOCK, N_KV_HEADS, GROUP, HEAD_DIM)
    pad = ((0, 0), (BLOCK, 0), (0, 0), (0, 0))
    kb = jnp.pad(k, pad).reshape(bsz, nb + 1, BLOCK, N_KV_HEADS, HEAD_DIM)
    vb = jnp.pad(v, pad).reshape(bsz, nb + 1, BLOCK, N_KV_HEADS, HEAD_DIM)
    kw = jnp.concatenate([kb[:, :-1], kb[:, 1:]], axis=2)
    vw = jnp.concatenate([vb[:, :-1], vb[:, 1:]], axis=2)
    s = jnp.einsum('bnqhgd,bnkhd->bnhgqk', qb, kw,
                   preferred_element_type=jnp.float32) * (HEAD_DIM ** -0.5)
    q_idx = jnp.arange(BLOCK)[:, None]
    k_idx = jnp.arange(2 * BLOCK)[None, :]
    diff = q_idx + BLOCK - k_idx
    band = (diff >= 0) & (diff < WINDOW)
    key_valid = (jnp.arange(nb)[:, None] * BLOCK - BLOCK + k_idx) >= 0
    mask = band[None, :, :] & key_valid[:, None, :]
    s = jnp.where(mask[None, :, None, None], s, -jnp.inf)
    sink = sinks.astype(jnp.float32).reshape(N_KV_HEADS, GROUP)[None, None, :, :, None, None]
    m = jnp.maximum(jnp.max(s, axis=-1, keepdims=True), sink)
    p = jnp.exp(s - m)
    denom = jnp.sum(p, axis=-1, keepdims=True) + jnp.exp(sink - m)
    o = jnp.einsum('bnhgqk,bnkhd->bnqhgd', (p / denom).astype(v.dtype), vw)
    return o.reshape(bsz, s_len, N_Q_HEADS * HEAD_DIM)


def conv_attn_mixer(h, positions, w_in, a_conv_w, a_conv_b, a_ln_g, a_ln_b, sinks, w_out):
    bsz, s_len = h.shape[0], h.shape[1]
    z = h @ w_in
    a_lin, a_gate, q, k, v = jnp.split(
        z, [A_CH, 2 * A_CH, 2 * A_CH + Q_DIM, 2 * A_CH + Q_DIM + KV_DIM], axis=-1)
    a = a_lin * jax.nn.sigmoid(a_gate)
    a = causal_dwconv(a, a_conv_w) + a_conv_b
    a = jax.nn.silu(layer_norm(a, a_ln_g, a_ln_b))
    q = partial_rope(q.reshape(bsz, s_len, N_Q_HEADS, HEAD_DIM), positions)
    k = partial_rope(k.reshape(bsz, s_len, N_KV_HEADS, HEAD_DIM), positions)
    v = v.reshape(bsz, s_len, N_KV_HEADS, HEAD_DIM)
    o = sliding_window_attention(q, k, v, sinks)
    return jnp.concatenate([a, o], axis=-1) @ w_out


def short_conv_mixer(h, w_in, conv_w, w_out):
    z = h @ w_in
    b_gate, c_gate, u = jnp.split(z, 3, axis=-1)
    y = b_gate * causal_dwconv(c_gate * u, conv_w)
    return y @ w_out


def conv_glu_ffn(h, w_up, conv_w, w_down):
    u = causal_dwconv(h @ w_up, conv_w)
    g, val = jnp.split(u, 2, axis=-1)
    return (jax.nn.silu(g) * val) @ w_down


def setup_inputs(seed: int = 0) -> dict:
    key = jax.random.key(seed)
    ks = jax.random.split(key, 20)

    def nrm(k, shape, scale):
        return jax.random.normal(k, shape, jnp.float32) * scale

    def gain(k, shape):
        return 1.0 + 0.05 * jax.random.normal(k, shape, jnp.float32)

    x = nrm(ks[0], (BATCH, SEQ, D_MODEL), 1.0)
    offsets = jax.random.randint(ks[1], (BATCH, 1), 0, 4096, dtype=jnp.int32)
    positions = offsets + jnp.arange(SEQ, dtype=jnp.int32)[None, :]
    return {
        'x': x,
        'positions': positions,
        'mix_norm_pre': gain(ks[2], (DEPTH, D_MODEL)),
        'mix_norm_post': gain(ks[3], (DEPTH, D_MODEL)),
        'ffn_norm_pre': gain(ks[4], (DEPTH, D_MODEL)),
        'ffn_norm_post': gain(ks[5], (DEPTH, D_MODEL)),
        'ev_w_in': nrm(ks[6], (N_EVEN, D_MODEL, EVEN_IN), D_MODEL ** -0.5),
        'ev_a_conv_w': nrm(ks[7], (N_EVEN, A_CONV, A_CH), A_CONV ** -0.5),
        'ev_a_conv_b': nrm(ks[8], (N_EVEN, A_CH), 0.02),
        'ev_a_ln_g': gain(ks[9], (N_EVEN, A_CH)),
        'ev_a_ln_b': nrm(ks[10], (N_EVEN, A_CH), 0.02),
        'ev_sinks': nrm(ks[11], (N_EVEN, N_Q_HEADS), 1.0),
        'ev_w_out': nrm(ks[12], (N_EVEN, MIX_DIM, D_MODEL), MIX_DIM ** -0.5),
        'od_w_in': nrm(ks[13], (N_ODD, D_MODEL, 3 * SC_DIM), D_MODEL ** -0.5),
        'od_conv_w': nrm(ks[14], (N_ODD, SC_CONV, SC_DIM), SC_CONV ** -0.5),
        'od_w_out': nrm(ks[15], (N_ODD, SC_DIM, D_MODEL), SC_DIM ** -0.5),
        'ffn_w_up': nrm(ks[16], (DEPTH, D_MODEL, 2 * D_FF), D_MODEL ** -0.5),
        'ffn_conv_w': nrm(ks[17], (DEPTH, FFN_CONV, 2 * D_FF), FFN_CONV ** -0.5),
        'ffn_w_down': nrm(ks[18], (DEPTH, D_FF, D_MODEL), D_FF ** -0.5),
    }


def reference(x, positions, mix_norm_pre, mix_norm_post, ffn_norm_pre, ffn_norm_post,
              ev_w_in, ev_a_conv_w, ev_a_conv_b, ev_a_ln_g, ev_a_ln_b, ev_sinks, ev_w_out,
              od_w_in, od_conv_w, od_w_out, ffn_w_up, ffn_conv_w, ffn_w_down):
    for i in range(DEPTH):
        j = i // 2
        h = rms_norm(x, mix_norm_pre[i])
        if i % 2 == 0:
            m = conv_attn_mixer(h, positions, ev_w_in[j], ev_a_conv_w[j], ev_a_conv_b[j],
                                ev_a_ln_g[j], ev_a_ln_b[j], ev_sinks[j], ev_w_out[j])
        else:
            m = short_conv_mixer(h, od_w_in[j], od_conv_w[j], od_w_out[j])
        x = x + rms_norm(m, mix_norm_post[i])
        h = rms_norm(x, ffn_norm_pre[i])
        f = conv_glu_ffn(h, ffn_w_up[i], ffn_conv_w[i], ffn_w_down[i])
        x = x + rms_norm(f, ffn_norm_post[i])
    return x
```

```python
import functools

import jax
import jax.numpy as jnp
from jax import lax
from jax.experimental import pallas as pl
from jax.experimental.pallas import tpu as pltpu

D_MODEL = 1024
HEAD_DIM = 64
A_CH = 512
A_CONV = 31
N_Q_HEADS = 8
N_KV_HEADS = 2
GROUP = 4
WINDOW = 128
ROPE_THETA = 500000.0
ROPE_DIM = 16
ROPE_HALF = ROPE_DIM // 2
Q_DIM = 512
KV_DIM = 128
EVEN_IN = 1792
SC_DIM = 1024
D_FF = 2816
RMS_EPS = 1e-6
LN_EPS = 1e-5

LANES = 128
SUBLANES = 8
VMEM_BYTES = 64 << 20

SEQ_TILE = 512
FF_CHUNK = 256
CONV_ROWS = 64
A_HALO = 32


def _rows(ref, start, size):
    return ref[:, pl.ds(0, 1, stride=2), pl.ds(start, size), :]


def _rms(x, g):
    ms = jnp.mean(x * x, axis=-1, keepdims=True)
    return x * lax.rsqrt(ms + RMS_EPS) * g


def _to_slabs(ref, row0, val):
    rows = val.shape[0]
    for k in range(val.shape[1] // LANES):
        ref[k, 0, row0:row0 + rows, :] = val[:, k * LANES:(k + 1) * LANES]


def _from_slabs(val):
    return jnp.concatenate([val[k, 0] for k in range(val.shape[0])], axis=-1)


def _conv3(ubuf, tm, w):
    u0 = ubuf[:, :, SUBLANES:SUBLANES + tm, :]
    u1 = _rows(ubuf, SUBLANES - 1, tm)
    u2 = _rows(ubuf, SUBLANES - 2, tm)
    return w[2] * u0 + w[1] * u1 + w[0] * u2


def _rope_kernel(pos_ref, invf_ref, cos_ref, sin_ref):
    ang = pos_ref[...] * invf_ref[...]
    cos_ref[...] = jnp.cos(ang)
    sin_ref[...] = jnp.sin(ang)


def _rope_tables(positions):
    b, s = positions.shape
    inv_freq = ROPE_THETA ** (-(jnp.arange(ROPE_HALF, dtype=jnp.float32) * 2.0 / ROPE_DIM))
    pos = jnp.repeat(positions.astype(jnp.float32).reshape(-1), ROPE_HALF).reshape(-1, LANES)
    invf = jnp.tile(inv_freq, LANES // ROPE_HALF).reshape(1, LANES)
    cos, sin = pl.pallas_call(
        _rope_kernel,
        out_shape=(jax.ShapeDtypeStruct(pos.shape, jnp.float32),) * 2,
        name="rope_tables",
    )(pos, invf)
    cos = cos.reshape(b, s, ROPE_HALF)
    sin = sin.reshape(b, s, ROPE_HALF)
    one = jnp.ones((b, s, HEAD_DIM - ROPE_DIM), jnp.float32)
    zero = jnp.zeros((b, s, HEAD_DIM - ROPE_DIM), jnp.float32)
    z8 = jnp.zeros_like(sin)
    c = jnp.concatenate([cos, cos, one] * 2, axis=-1)
    s_up = jnp.concatenate([-sin, z8, zero] * 2, axis=-1)
    s_dn = jnp.concatenate([z8, sin, zero] * 2, axis=-1)
    return c, s_up, s_dn


def _ffn_kernel(x_ref, gpre_ref, gpost_ref, wup_ref, cw_ref, wdown_ref, o_ref,
                ubuf, carry, acc_ref, *, tm):
    i = pl.program_id(1)
    nk = FF_CHUNK // LANES
    nslab_half = D_FF // LANES

    @pl.when(i == 0)
    def _():
        carry[...] = jnp.zeros_like(carry)

    x = x_ref[...]
    h = _rms(x, gpre_ref[...]).astype(jnp.bfloat16)
    for c in range(D_FF // FF_CHUNK):
        col = c * FF_CHUNK
        ug = jnp.dot(h, wup_ref[:, col:col + FF_CHUNK], preferred_element_type=jnp.float32)
        uv = jnp.dot(h, wup_ref[:, D_FF + col:D_FF + col + FF_CHUNK],
                     preferred_element_type=jnp.float32)
        sg = c * nk
        sv = nslab_half + c * nk
        ubuf[0:nk, :, 0:SUBLANES, :] = carry[sg:sg + nk]
        ubuf[nk:2 * nk, :, 0:SUBLANES, :] = carry[sv:sv + nk]
        _to_slabs(ubuf.at[0:nk], SUBLANES, ug)
        _to_slabs(ubuf.at[nk:2 * nk], SUBLANES, uv)
        carry[sg:sg + nk] = ubuf[0:nk, :, tm:tm + SUBLANES, :]
        carry[sv:sv + nk] = ubuf[nk:2 * nk, :, tm:tm + SUBLANES, :]
        w = jnp.concatenate([cw_ref[:, sg:sg + nk], cw_ref[:, sv:sv + nk]], axis=1)
        y = _conv3(ubuf, tm, w)
        yg = y[0:nk]
        act = (yg * jax.nn.sigmoid(yg) * y[nk:2 * nk]).astype(jnp.bfloat16)
        part = jnp.dot(_from_slabs(act), wdown_ref[col:col + FF_CHUNK, :],
                       preferred_element_type=jnp.float32)
        if c == 0:
            acc_ref[...] = part
        else:
            acc_ref[...] += part
    o_ref[...] = x + _rms(acc_ref[...], gpost_ref[...])


def _const_spec(shape):
    nd = len(shape)
    return pl.BlockSpec(shape, lambda b, i: (0,) * nd, pipeline_mode=pl.Buffered(1))


def _x_spec(tm, width):
    return pl.BlockSpec((None, tm, width), lambda b, i: (b, i, 0))


def _vmem_limit(resident_bytes):
    return int(min(VMEM_BYTES - (4 << 20), 2 * resident_bytes + (8 << 20)))


def _ffn(x, g_pre, g_post, w_up, conv_w, w_down):
    b, s, d = x.shape
    tm = SEQ_TILE
    nslab = 2 * D_FF // LANES
    nk = FF_CHUNK // LANES
    cw = conv_w.reshape(3, nslab, 1, 1, LANES)
    resident = (4 * tm * d * 4 + w_up.size * 2 + w_down.size * 2 + tm * d * 4
                + 2 * nk * (tm + SUBLANES) * LANES * 4)
    return pl.pallas_call(
        functools.partial(_ffn_kernel, tm=tm),
        grid=(b, s // tm),
        in_specs=[
            _x_spec(tm, d),
            _const_spec((1, d)),
            _const_spec((1, d)),
            _const_spec(w_up.shape),
            _const_spec(cw.shape),
            _const_spec(w_down.shape),
        ],
        out_specs=_x_spec(tm, d),
        out_shape=jax.ShapeDtypeStruct(x.shape, x.dtype),
        scratch_shapes=[
            pltpu.VMEM((2 * nk, 1, tm + SUBLANES, LANES), jnp.float32),
            pltpu.VMEM((nslab, 1, SUBLANES, LANES), jnp.float32),
            pltpu.VMEM((tm, d), jnp.float32),
        ],
        compiler_params=pltpu.CompilerParams(
            dimension_semantics=("arbitrary", "arbitrary"),
            vmem_limit_bytes=_vmem_limit(resident)),
        name="ffn",
    )(x, g_pre.reshape(1, d), g_post.reshape(1, d), w_up, cw, w_down)


def _odd_kernel(x_ref, gpre_ref, gpost_ref, win_ref, cw_ref, wout_ref, o_ref,
                ubuf, carry, acc_ref, *, tm):
    i = pl.program_id(1)
    nk = FF_CHUNK // LANES

    @pl.when(i == 0)
    def _():
        carry[...] = jnp.zeros_like(carry)

    x = x_ref[...]
    h = _rms(x, gpre_ref[...]).astype(jnp.bfloat16)
    for c in range(SC_DIM // FF_CHUNK):
        col = c * FF_CHUNK
        zb = jnp.dot(h, win_ref[:, col:col + FF_CHUNK], preferred_element_type=jnp.float32)
        zc = jnp.dot(h, win_ref[:, SC_DIM + col:SC_DIM + col + FF_CHUNK],
                     preferred_element_type=jnp.float32)
        zu = jnp.dot(h, win_ref[:, 2 * SC_DIM + col:2 * SC_DIM + col + FF_CHUNK],
                     preferred_element_type=jnp.float32)
        s0 = c * nk
        ubuf[:, :, 0:SUBLANES, :] = carry[s0:s0 + nk]
        _to_slabs(ubuf, SUBLANES, zc * zu)
        carry[s0:s0 + nk] = ubuf[:, :, tm:tm + SUBLANES, :]
        y = _conv3(ubuf, tm, cw_ref[:, s0:s0 + nk])
        y = (zb * _from_slabs(y)).astype(jnp.bfloat16)
        part = jnp.dot(y, wout_ref[col:col + FF_CHUNK, :], preferred_element_type=jnp.float32)
        if c == 0:
            acc_ref[...] = part
        else:
            acc_ref[...] += part
    o_ref[...] = x + _rms(acc_ref[...], gpost_ref[...])


def _odd_mixer(x, g_pre, g_post, w_in, conv_w, w_out):
    b, s, d = x.shape
    tm = SEQ_TILE
    nslab = SC_DIM // LANES
    nk = FF_CHUNK // LANES
    cw = conv_w.reshape(3, nslab, 1, 1, LANES)
    resident = (4 * tm * d * 4 + w_in.size * 2 + w_out.size * 2 + tm * d * 4
                + nk * (tm + SUBLANES) * LANES * 4)
    return pl.pallas_call(
        functools.partial(_odd_kernel, tm=tm),
        grid=(b, s // tm),
        in_specs=[
            _x_spec(tm, d),
            _const_spec((1, d)),
            _const_spec((1, d)),
            _const_spec(w_in.shape),
            _const_spec(cw.shape),
            _const_spec(w_out.shape),
        ],
        out_specs=_x_spec(tm, d),
        out_shape=jax.ShapeDtypeStruct(x.shape, x.dtype),
        scratch_shapes=[
            pltpu.VMEM((nk, 1, tm + SUBLANES, LANES), jnp.float32),
            pltpu.VMEM((nslab, 1, SUBLANES, LANES), jnp.float32),
            pltpu.VMEM((tm, d), jnp.float32),
        ],
        compiler_params=pltpu.CompilerParams(
            dimension_semantics=("arbitrary", "arbitrary"),
            vmem_limit_bytes=_vmem_limit(resident)),
        name="odd_mixer",
    )(x, g_pre.reshape(1, d), g_post.reshape(1, d), w_in, cw, w_out)


def _rope(x, c, s_up, s_dn):
    out = []
    for k in range(x.shape[1] // LANES):
        xk = x[:, k * LANES:(k + 1) * LANES]
        up = pltpu.roll(xk, LANES - ROPE_HALF, axis=1)
        dn = pltpu.roll(xk, ROPE_HALF, axis=1)
        out.append(xk * c + up * s_up + dn * s_dn)
    return jnp.concatenate(out, axis=-1)


def _even_kernel(x_ref, c_ref, sup_ref, sdn_ref, gpre_ref, gpost_ref, win_ref, cw_ref, cb_ref,
                 lng_ref, lnb_ref, sink_ref, wout_ref, o_ref,
                 abuf, ybuf, kc_ref, vc_ref, *, tm):
    i = pl.program_id(1)
    nblk = tm // WINDOW
    nslab = A_CH // LANES

    @pl.when(i == 0)
    def _():
        abuf[:, :, 0:A_HALO, :] = jnp.zeros((nslab, 1, A_HALO, LANES), jnp.float32)
        kc_ref[...] = jnp.zeros_like(kc_ref)
        vc_ref[...] = jnp.zeros_like(vc_ref)

    @pl.when(i > 0)
    def _():
        abuf[:, :, 0:A_HALO, :] = abuf[:, :, tm:tm + A_HALO, :]

    x = x_ref[...]
    h = _rms(x, gpre_ref[...]).astype(jnp.bfloat16)

    za = jnp.dot(h, win_ref[:, 0:2 * A_CH], preferred_element_type=jnp.float32)
    _to_slabs(abuf, A_HALO, za[:, 0:A_CH] * jax.nn.sigmoid(za[:, A_CH:2 * A_CH]))
    cw = cw_ref[...]
    cb = cb_ref[...]

    def conv_step(r, carry):
        base = pl.multiple_of(r * CONV_ROWS, CONV_ROWS)
        acc = jnp.broadcast_to(cb, (nslab, 1, CONV_ROWS, LANES))
        for j in range(A_CONV):
            acc = acc + _rows(abuf, base + (A_HALO - A_CONV + 1) + j, CONV_ROWS) * cw[j]
        for k in range(nslab):
            ybuf[pl.ds(base, CONV_ROWS), k * LANES:(k + 1) * LANES] = acc[k, 0]
        return carry

    lax.fori_loop(0, tm // CONV_ROWS, conv_step, 0)
    y = ybuf[...]
    mu = jnp.mean(y, axis=-1, keepdims=True)
    yc = y - mu
    y = yc * lax.rsqrt(jnp.mean(yc * yc, axis=-1, keepdims=True) + LN_EPS)
    y = y * lng_ref[...] + lnb_ref[...]
    a_out = (y * jax.nn.sigmoid(y)).astype(jnp.bfloat16)

    zq = jnp.dot(h, win_ref[:, 2 * A_CH:EVEN_IN], preferred_element_type=jnp.float32)
    c, s_up, s_dn = c_ref[...], sup_ref[...], sdn_ref[...]
    q = _rope(zq[:, 0:Q_DIM], c, s_up, s_dn).astype(jnp.bfloat16)
    k = _rope(zq[:, Q_DIM:Q_DIM + KV_DIM], c, s_up, s_dn).astype(jnp.bfloat16)
    v = zq[:, Q_DIM + KV_DIM:Q_DIM + 2 * KV_DIM].astype(jnp.bfloat16)
    kext = jnp.concatenate([kc_ref[...], k], axis=0)
    vext = jnp.concatenate([vc_ref[...], v], axis=0)
    kc_ref[...] = k[tm - WINDOW:tm]
    vc_ref[...] = v[tm - WINDOW:tm]
    lane = lax.broadcasted_iota(jnp.int32, kext.shape, 1)
    zero = jnp.zeros_like(kext)
    kg = (jnp.where(lane < HEAD_DIM, kext, zero), jnp.where(lane >= HEAD_DIM, kext, zero))
    vg = (jnp.where(lane < HEAD_DIM, vext, zero), jnp.where(lane >= HEAD_DIM, vext, zero))

    rows = GROUP * WINDOW
    r = lax.broadcasted_iota(jnp.int32, (rows, 2 * WINDOW), 0) & (WINDOW - 1)
    col = lax.broadcasted_iota(jnp.int32, (rows, 2 * WINDOW), 1)
    band = (col > r) & (col <= r + WINDOW)
    neg_inf = jnp.float32(-jnp.inf)
    bias = jnp.where(band, jnp.float32(0.0), neg_inf)
    first_valid_col = jnp.where(i > 0, 0, WINDOW)
    bias_first = jnp.where(band & (col >= first_valid_col), jnp.float32(0.0), neg_inf)

    o_blocks = []
    for j in range(nblk):
        qs = jnp.concatenate(
            [q[j * WINDOW:(j + 1) * WINDOW, p * LANES:(p + 1) * LANES] for p in range(GROUP)],
            axis=0)
        o = None
        for g in range(N_KV_HEADS):
            kw = kg[g][j * WINDOW:(j + 2) * WINDOW]
            vw = vg[g][j * WINDOW:(j + 2) * WINDOW]
            sc = lax.dot_general(qs, kw, (((1,), (1,)), ((), ())),
                                 preferred_element_type=jnp.float32)
            sc = sc + (bias_first if j == 0 else bias)
            sink = sink_ref[g]
            m = jnp.maximum(jnp.max(sc, axis=-1, keepdims=True), sink)
            p_un = jnp.exp(sc - m)
            denom = jnp.sum(p_un, axis=-1, keepdims=True) + jnp.exp(sink - m)
            og = jnp.dot(p_un.astype(jnp.bfloat16), vw, preferred_element_type=jnp.float32)
            og = og * (1.0 / denom)
            o = og if o is None else o + og
        o_blocks.append(jnp.concatenate(
            [o[p * WINDOW:(p + 1) * WINDOW] for p in range(GROUP)], axis=-1))
    o_all = jnp.concatenate(o_blocks, axis=0).astype(jnp.bfloat16)

    mix = jnp.concatenate([a_out, o_all], axis=-1)
    m_out = jnp.dot(mix, wout_ref[...], preferred_element_type=jnp.float32)
    o_ref[...] = x + _rms(m_out, gpost_ref[...])


def _pair_perm():
    idx = []
    for p in range(GROUP):
        for half in range(N_KV_HEADS):
            head = p + GROUP * half
            idx.extend(range(head * HEAD_DIM, (head + 1) * HEAD_DIM))
    return jnp.array(idx, dtype=jnp.int32)


def _even_mixer(x, tables, g_pre, g_post, w_in, conv_w, conv_b, ln_g, ln_b, sinks, w_out):
    b, s, d = x.shape
    tm = SEQ_TILE
    nslab = A_CH // LANES
    perm = _pair_perm()
    wq = w_in[:, 2 * A_CH:2 * A_CH + Q_DIM][:, perm] * (HEAD_DIM ** -0.5)
    w_in_p = jnp.concatenate([w_in[:, :2 * A_CH], wq, w_in[:, 2 * A_CH + Q_DIM:]],
                             axis=1).astype(jnp.bfloat16)
    w_out_p = jnp.concatenate([w_out[:A_CH], w_out[A_CH:][perm]], axis=0).astype(jnp.bfloat16)
    cw = jnp.pad(conv_w, ((0, 1), (0, 0))).reshape(A_CONV + 1, nslab, 1, 1, LANES)
    cb = conv_b.reshape(nslab, 1, 1, LANES)
    sink_rows = jnp.repeat(sinks.reshape(N_KV_HEADS, GROUP), WINDOW, axis=1)[..., None]
    c, s_up, s_dn = tables
    tab_spec = pl.BlockSpec((None, tm, LANES), lambda bb, i: (bb, i, 0))
    resident = (4 * tm * d * 4 + 6 * tm * LANES * 4 + w_in_p.size * 2 + w_out_p.size * 2
                + nslab * (tm + A_HALO) * LANES * 4 + tm * A_CH * 4)
    return pl.pallas_call(
        functools.partial(_even_kernel, tm=tm),
        grid=(b, s // tm),
        in_specs=[
            _x_spec(tm, d), tab_spec, tab_spec, tab_spec,
            _const_spec((1, d)),
            _const_spec((1, d)),
            _const_spec(w_in_p.shape),
            _const_spec(cw.shape),
            _const_spec(cb.shape),
            _const_spec((1, A_CH)),
            _const_spec((1, A_CH)),
            _const_spec(sink_rows.shape),
            _const_spec(w_out_p.shape),
        ],
        out_specs=_x_spec(tm, d),
        out_shape=jax.ShapeDtypeStruct(x.shape, x.dtype),
        scratch_shapes=[
            pltpu.VMEM((nslab, 1, tm + A_HALO, LANES), jnp.float32),
            pltpu.VMEM((tm, A_CH), jnp.float32),
            pltpu.VMEM((WINDOW, KV_DIM), jnp.bfloat16),
            pltpu.VMEM((WINDOW, KV_DIM), jnp.bfloat16),
        ],
        compiler_params=pltpu.CompilerParams(
            dimension_semantics=("arbitrary", "arbitrary"),
            vmem_limit_bytes=_vmem_limit(resident)),
        name="even_mixer",
    )(x, c, s_up, s_dn, g_pre.reshape(1, d), g_post.reshape(1, d), w_in_p, cw, cb,
      ln_g.reshape(1, A_CH), ln_b.reshape(1, A_CH), sink_rows, w_out_p)


def kernel(x, positions, mix_norm_pre, mix_norm_post, ffn_norm_pre, ffn_norm_post, ev_w_in, ev_a_conv_w, ev_a_conv_b, ev_a_ln_g, ev_a_ln_b, ev_sinks, ev_w_out, od_w_in, od_conv_w, od_w_out, ffn_w_up, ffn_conv_w, ffn_w_down):
    depth = mix_norm_pre.shape[0]
    assert x.shape[1] % SEQ_TILE == 0 and SEQ_TILE % WINDOW == 0
    tables = _rope_tables(positions)
    bf = jnp.bfloat16
    for i in range(depth):
        j = i // 2
        if i % 2 == 0:
            x = _even_mixer(x, tables, mix_norm_pre[i], mix_norm_post[i], ev_w_in[j],
                            ev_a_conv_w[j], ev_a_conv_b[j], ev_a_ln_g[j], ev_a_ln_b[j],
                            ev_sinks[j], ev_w_out[j])
        else:
            x = _odd_mixer(x, mix_norm_pre[i], mix_norm_post[i], od_w_in[j].astype(bf),
                           od_conv_w[j], od_w_out[j].astype(bf))
        x = _ffn(x, ffn_norm_pre[i], ffn_norm_post[i], ffn_w_up[i].astype(bf),
                 ffn_conv_w[i], ffn_w_down[i].astype(bf))
    return x
```

```python
import functools

import jax
import jax.numpy as jnp
import numpy as np
from jax import lax
from jax.experimental import pallas as pl
from jax.experimental.pallas import tpu as pltpu

D_MODEL = 1024
HEAD_DIM = 64
A_CH = 512
A_CONV = 31
N_Q_HEADS = 8
N_KV_HEADS = 2
GROUP = 4
WINDOW = 128
ROPE_THETA = 500000.0
ROPE_DIM = 16
ROPE_HALF = ROPE_DIM // 2
Q_DIM = 512
KV_DIM = 128
EVEN_IN = 1792
SC_DIM = 1024
D_FF = 2816
RMS_EPS = 1e-6
LN_EPS = 1e-5

LANES = 128
SUBLANES = 8
VMEM_BYTES = 64 << 20

SEQ_TILE = 512
FF_CHUNK = 256
CONV_ROWS = 64
ROW_BLK = 64
A_HALO = 32


def _rows(ref, start, size):
    return ref[:, pl.ds(0, 1, stride=2), pl.ds(start, size), :]


def _rms(x, g):
    ms = jnp.mean(x * x, axis=-1, keepdims=True)
    return x * lax.rsqrt(ms + RMS_EPS) * g


def _to_slabs(ref, row0, val):
    rows = val.shape[0]
    for k in range(val.shape[1] // LANES):
        ref[k, 0, row0:row0 + rows, :] = val[:, k * LANES:(k + 1) * LANES]


def _from_slabs(val):
    return jnp.concatenate([val[k, 0] for k in range(val.shape[0])], axis=-1)


def _conv3(ubuf, r0, rows, w):
    u0 = ubuf[:, :, SUBLANES + r0:SUBLANES + r0 + rows, :]
    u1 = _rows(ubuf, SUBLANES - 1 + r0, rows)
    u2 = _rows(ubuf, SUBLANES - 2 + r0, rows)
    return w[2] * u0 + w[1] * u1 + w[0] * u2


def _prenorm_rows(x_ref, g_ref, h_ref):
    g = g_ref[...]
    for r0 in range(0, x_ref.shape[0], ROW_BLK):
        h_ref[r0:r0 + ROW_BLK, :] = _rms(x_ref[r0:r0 + ROW_BLK, :], g).astype(jnp.bfloat16)


def _residual_rows(x_ref, f_ref, g_ref, o_ref):
    g = g_ref[...]
    for r0 in range(0, x_ref.shape[0], ROW_BLK):
        o_ref[r0:r0 + ROW_BLK, :] = (x_ref[r0:r0 + ROW_BLK, :]
                                     + _rms(f_ref[r0:r0 + ROW_BLK, :], g))


def _const_spec(shape):
    nd = len(shape)
    return pl.BlockSpec(shape, lambda b, i: (0,) * nd, pipeline_mode=pl.Buffered(1))


def _x_spec(tm, width):
    return pl.BlockSpec((None, tm, width), lambda b, i: (b, i, 0))


def _vmem_limit(resident_bytes):
    return int(min(VMEM_BYTES - (4 << 20), 2 * resident_bytes + (8 << 20)))


def _rope_kernel(pos_ref, invf_ref, cos_ref, sin_ref):
    ang = pos_ref[...] * invf_ref[...]
    cos_ref[...] = jnp.cos(ang)
    sin_ref[...] = jnp.sin(ang)


def _rope_tables(positions):
    b, s = positions.shape
    inv_freq = ROPE_THETA ** (-(jnp.arange(ROPE_HALF, dtype=jnp.float32) * 2.0 / ROPE_DIM))
    pos = jnp.repeat(positions.astype(jnp.float32).reshape(-1), ROPE_HALF).reshape(-1, LANES)
    invf = jnp.tile(inv_freq, LANES // ROPE_HALF).reshape(1, LANES)
    cos, sin = pl.pallas_call(
        _rope_kernel,
        out_shape=(jax.ShapeDtypeStruct(pos.shape, jnp.float32),) * 2,
        name="rope_tables",
    )(pos, invf)
    return jnp.concatenate([cos.reshape(b, s, ROPE_HALF), sin.reshape(b, s, ROPE_HALF)], axis=-1)


def _rope_expand_matrix():
    e = np.zeros((ROPE_DIM, 3 * LANES), np.float32)
    for lane in range(LANES):
        d = lane % HEAD_DIM
        if d < ROPE_HALF:
            e[d, lane] = 1.0
            e[ROPE_HALF + d, LANES + lane] = -1.0
        elif d < ROPE_DIM:
            e[d - ROPE_HALF, lane] = 1.0
            e[d, 2 * LANES + lane] = 1.0
    return e


def _ffn_kernel(x_ref, gpre_ref, gpost_ref, wup_ref, cw_ref, wdown_ref, o_ref,
                ubuf, carry, h_ref, act_ref, f_ref, *, tm):
    i = pl.program_id(1)
    nk = FF_CHUNK // LANES
    nslab_half = D_FF // LANES

    @pl.when(i == 0)
    def _():
        carry[...] = jnp.zeros_like(carry)

    _prenorm_rows(x_ref, gpre_ref, h_ref)
    for c in range(D_FF // FF_CHUNK):
        col = c * FF_CHUNK
        ug = jnp.dot(h_ref[...], wup_ref[:, col:col + FF_CHUNK],
                     preferred_element_type=jnp.float32)
        uv = jnp.dot(h_ref[...], wup_ref[:, D_FF + col:D_FF + col + FF_CHUNK],
                     preferred_element_type=jnp.float32)
        sg = c * nk
        sv = nslab_half + c * nk
        ub = ubuf.at[c % 2]
        ub[0:nk, :, 0:SUBLANES, :] = carry[sg:sg + nk]
        ub[nk:2 * nk, :, 0:SUBLANES, :] = carry[sv:sv + nk]
        _to_slabs(ub.at[0:nk], SUBLANES, ug)
        _to_slabs(ub.at[nk:2 * nk], SUBLANES, uv)
        carry[sg:sg + nk] = ub[0:nk, :, tm:tm + SUBLANES, :]
        carry[sv:sv + nk] = ub[nk:2 * nk, :, tm:tm + SUBLANES, :]
        w = jnp.concatenate([cw_ref[:, sg:sg + nk], cw_ref[:, sv:sv + nk]], axis=1)
        for r0 in range(0, tm, ROW_BLK):
            y = _conv3(ub, r0, ROW_BLK, w)
            yg = y[0:nk]
            act = (yg * jax.nn.sigmoid(yg) * y[nk:2 * nk]).astype(jnp.bfloat16)
            act_ref[r0:r0 + ROW_BLK, col:col + FF_CHUNK] = _from_slabs(act)
    f_ref[...] = jnp.dot(act_ref[...], wdown_ref[...], preferred_element_type=jnp.float32)
    _residual_rows(x_ref, f_ref, gpost_ref, o_ref)


def _ffn(x, g_pre, g_post, w_up, conv_w, w_down):
    b, s, d = x.shape
    tm = SEQ_TILE
    nslab = 2 * D_FF // LANES
    nk = FF_CHUNK // LANES
    cw = conv_w.reshape(3, nslab, 1, 1, LANES)
    resident = (4 * tm * d * 4 + w_up.size * 2 + w_down.size * 2 + tm * D_FF * 2
                + tm * d * 6 + 4 * nk * (tm + SUBLANES) * LANES * 4)
    return pl.pallas_call(
        functools.partial(_ffn_kernel, tm=tm),
        grid=(b, s // tm),
        in_specs=[
            _x_spec(tm, d),
            _const_spec((1, d)),
            _const_spec((1, d)),
            _const_spec(w_up.shape),
            _const_spec(cw.shape),
            _const_spec(w_down.shape),
        ],
        out_specs=_x_spec(tm, d),
        out_shape=jax.ShapeDtypeStruct(x.shape, x.dtype),
        scratch_shapes=[
            pltpu.VMEM((2, 2 * nk, 1, tm + SUBLANES, LANES), jnp.float32),
            pltpu.VMEM((nslab, 1, SUBLANES, LANES), jnp.float32),
            pltpu.VMEM((tm, d), jnp.bfloat16),
            pltpu.VMEM((tm, D_FF), jnp.bfloat16),
            pltpu.VMEM((tm, d), jnp.float32),
        ],
        compiler_params=pltpu.CompilerParams(
            dimension_semantics=("arbitrary", "arbitrary"),
            vmem_limit_bytes=_vmem_limit(resident)),
        name="ffn",
    )(x, g_pre.reshape(1, d), g_post.reshape(1, d), w_up, cw, w_down)


def _odd_kernel(x_ref, gpre_ref, gpost_ref, win_ref, cw_ref, wout_ref, o_ref,
                ubuf, carry, h_ref, y_ref, f_ref, *, tm):
    i = pl.program_id(1)
    nk = FF_CHUNK // LANES

    @pl.when(i == 0)
    def _():
        carry[...] = jnp.zeros_like(carry)

    _prenorm_rows(x_ref, gpre_ref, h_ref)
    for c in range(SC_DIM // FF_CHUNK):
        col = c * FF_CHUNK
        zb = jnp.dot(h_ref[...], win_ref[:, col:col + FF_CHUNK],
                     preferred_element_type=jnp.float32)
        zc = jnp.dot(h_ref[...], win_ref[:, SC_DIM + col:SC_DIM + col + FF_CHUNK],
                     preferred_element_type=jnp.float32)
        zu = jnp.dot(h_ref[...], win_ref[:, 2 * SC_DIM + col:2 * SC_DIM + col + FF_CHUNK],
                     preferred_element_type=jnp.float32)
        s0 = c * nk
        ub = ubuf.at[c % 2]
        ub[0:nk, :, 0:SUBLANES, :] = carry[s0:s0 + nk]
        _to_slabs(ub.at[0:nk], SUBLANES, zc * zu)
        _to_slabs(ub.at[nk:2 * nk], SUBLANES, zb)
        carry[s0:s0 + nk] = ub[0:nk, :, tm:tm + SUBLANES, :]
        w = cw_ref[:, s0:s0 + nk]
        for r0 in range(0, tm, ROW_BLK):
            y = _conv3(ub.at[0:nk], r0, ROW_BLK, w)
            b_gate = ub[nk:2 * nk, :, SUBLANES + r0:SUBLANES + r0 + ROW_BLK, :]
            y_ref[r0:r0 + ROW_BLK, col:col + FF_CHUNK] = _from_slabs(
                (b_gate * y).astype(jnp.bfloat16))
    f_ref[...] = jnp.dot(y_ref[...], wout_ref[...], preferred_element_type=jnp.float32)
    _residual_rows(x_ref, f_ref, gpost_ref, o_ref)


def _odd_mixer(x, g_pre, g_post, w_in, conv_w, w_out):
    b, s, d = x.shape
    tm = SEQ_TILE
    nslab = SC_DIM // LANES
    nk = FF_CHUNK // LANES
    cw = conv_w.reshape(3, nslab, 1, 1, LANES)
    resident = (4 * tm * d * 4 + w_in.size * 2 + w_out.size * 2 + tm * SC_DIM * 2
                + tm * d * 6 + 4 * nk * (tm + SUBLANES) * LANES * 4)
    return pl.pallas_call(
        functools.partial(_odd_kernel, tm=tm),
        grid=(b, s // tm),
        in_specs=[
            _x_spec(tm, d),
            _const_spec((1, d)),
            _const_spec((1, d)),
            _const_spec(w_in.shape),
            _const_spec(cw.shape),
            _const_spec(w_out.shape),
        ],
        out_specs=_x_spec(tm, d),
        out_shape=jax.ShapeDtypeStruct(x.shape, x.dtype),
        scratch_shapes=[
            pltpu.VMEM((2, 2 * nk, 1, tm + SUBLANES, LANES), jnp.float32),
            pltpu.VMEM((nslab, 1, SUBLANES, LANES), jnp.float32),
            pltpu.VMEM((tm, d), jnp.bfloat16),
            pltpu.VMEM((tm, SC_DIM), jnp.bfloat16),
            pltpu.VMEM((tm, d), jnp.float32),
        ],
        compiler_params=pltpu.CompilerParams(
            dimension_semantics=("arbitrary", "arbitrary"),
            vmem_limit_bytes=_vmem_limit(resident)),
        name="odd_mixer",
    )(x, g_pre.reshape(1, d), g_post.reshape(1, d), w_in, cw, w_out)


def _split3_bf16(x):
    hi = x.astype(jnp.bfloat16)
    r = x - hi.astype(jnp.float32)
    mid = r.astype(jnp.bfloat16)
    lo = (r - mid.astype(jnp.float32)).astype(jnp.bfloat16)
    return hi, mid, lo


def _rope(x, c, s_up, s_dn):
    up = pltpu.roll(x, LANES - ROPE_HALF, axis=1)
    dn = pltpu.roll(x, ROPE_HALF, axis=1)
    return x * c + up * s_up + dn * s_dn


def _even_kernel(x_ref, cs_ref, e_ref, gpre_ref, gpost_ref, win_ref, cw_ref, cb_ref,
                 lng_ref, lnb_ref, sink_ref, wout_ref, o_ref,
                 abuf, h_ref, q_ref, kg_ref, vg_ref, mix_ref, f_ref, *, tm):
    i = pl.program_id(1)
    nblk = tm // WINDOW
    nslab = A_CH // LANES

    @pl.when(i == 0)
    def _():
        abuf[:, :, 0:A_HALO, :] = jnp.zeros((nslab, 1, A_HALO, LANES), jnp.float32)
        kg_ref[:, 0:WINDOW, :] = jnp.zeros((N_KV_HEADS, WINDOW, LANES), jnp.bfloat16)
        vg_ref[:, 0:WINDOW, 0:LANES] = jnp.zeros((N_KV_HEADS, WINDOW, LANES), jnp.bfloat16)
        vg_ref[:, :, LANES:2 * LANES] = jnp.ones((N_KV_HEADS, tm + WINDOW, LANES), jnp.bfloat16)

    @pl.when(i > 0)
    def _():
        abuf[:, :, 0:A_HALO, :] = abuf[:, :, tm:tm + A_HALO, :]
        kg_ref[:, 0:WINDOW, :] = kg_ref[:, tm:tm + WINDOW, :]
        vg_ref[:, 0:WINDOW, 0:LANES] = vg_ref[:, tm:tm + WINDOW, 0:LANES]

    _prenorm_rows(x_ref, gpre_ref, h_ref)

    za = jnp.dot(h_ref[...], win_ref[:, 0:2 * A_CH], preferred_element_type=jnp.float32)
    _to_slabs(abuf, A_HALO, za[:, 0:A_CH] * jax.nn.sigmoid(za[:, A_CH:2 * A_CH]))

    e = e_ref[...]
    t = None
    for part in _split3_bf16(cs_ref[...]):
        d = jnp.dot(part, e, preferred_element_type=jnp.float32)
        t = d if t is None else t + d
    lane = lax.broadcasted_iota(jnp.int32, (tm, LANES), 1)
    c = t[:, 0:LANES] + jnp.where((lane & (HEAD_DIM - 1)) >= ROPE_DIM, 1.0, 0.0)
    s_up = t[:, LANES:2 * LANES]
    s_dn = t[:, 2 * LANES:3 * LANES]

    zq = jnp.dot(h_ref[...], win_ref[:, 2 * A_CH:EVEN_IN], preferred_element_type=jnp.float32)
    for p in range(GROUP):
        q_ref[:, p * LANES:(p + 1) * LANES] = _rope(
            zq[:, p * LANES:(p + 1) * LANES], c, s_up, s_dn).astype(jnp.bfloat16)
    k = _rope(zq[:, Q_DIM:Q_DIM + KV_DIM], c, s_up, s_dn).astype(jnp.bfloat16)
    v = zq[:, Q_DIM + KV_DIM:Q_DIM + 2 * KV_DIM].astype(jnp.bfloat16)
    zero = jnp.zeros_like(k)
    for g in range(N_KV_HEADS):
        mine = (lane >= g * HEAD_DIM) & (lane < (g + 1) * HEAD_DIM)
        kg_ref[g, WINDOW:WINDOW + tm, :] = jnp.where(mine, k, zero)
        vg_ref[g, WINDOW:WINDOW + tm, 0:LANES] = jnp.where(mine, v, zero)

    rows = GROUP * WINDOW
    r = lax.broadcasted_iota(jnp.int32, (rows, 2 * WINDOW), 0) & (WINDOW - 1)
    col = lax.broadcasted_iota(jnp.int32, (rows, 2 * WINDOW), 1)
    band = (col > r) & (col <= r + WINDOW)
    neg_inf = jnp.float32(-jnp.inf)
    bias = jnp.where(band, jnp.float32(0.0), neg_inf)
    first_valid_col = jnp.where(i > 0, 0, WINDOW)
    bias_first = jnp.where(band & (col >= first_valid_col), jnp.float32(0.0), neg_inf)

    def attend(j, g):
        qs = jnp.concatenate(
            [q_ref[j * WINDOW:(j + 1) * WINDOW, p * LANES:(p + 1) * LANES] for p in range(GROUP)],
            axis=0)
        kw = kg_ref[g, j * WINDOW:(j + 2) * WINDOW, :]
        vw = vg_ref[g, j * WINDOW:(j + 2) * WINDOW, :]
        sc = lax.dot_general(qs, kw, (((1,), (1,)), ((), ())),
                             preferred_element_type=jnp.float32)
        sc = sc + (bias_first if j == 0 else bias)
        sink = sink_ref[g]
        m = jnp.maximum(jnp.max(sc, axis=-1, keepdims=True), sink)
        p_un = jnp.exp(sc - jnp.concatenate([m, m], axis=-1))
        pv = jnp.dot(p_un.astype(jnp.bfloat16), vw, preferred_element_type=jnp.float32)
        denom = pv[:, LANES:2 * LANES] + jnp.exp(sink - m)
        return pv[:, 0:LANES] * (1.0 / denom)

    def conv_ln(r0):
        acc = jnp.broadcast_to(cb_ref[...], (nslab, 1, CONV_ROWS, LANES))
        for tap in range(A_CONV):
            acc = acc + _rows(abuf, r0 + (A_HALO - A_CONV + 1) + tap, CONV_ROWS) * cw_ref[tap]
        y = _from_slabs(acc)
        mu = jnp.mean(y, axis=-1, keepdims=True)
        yc = y - mu
        y = yc * lax.rsqrt(jnp.mean(yc * yc, axis=-1, keepdims=True) + LN_EPS)
        y = y * lng_ref[...] + lnb_ref[...]
        mix_ref[r0:r0 + CONV_ROWS, 0:A_CH] = (y * jax.nn.sigmoid(y)).astype(jnp.bfloat16)

    conv_per_unit = tm // CONV_ROWS // (nblk * N_KV_HEADS)
    unit = 0
    for j in range(nblk):
        o = None
        for g in range(N_KV_HEADS):
            og = attend(j, g)
            o = og if o is None else o + og
            for cc in range(conv_per_unit):
                conv_ln((unit * conv_per_unit + cc) * CONV_ROWS)
            unit += 1
        mix_ref[j * WINDOW:(j + 1) * WINDOW, A_CH:2 * A_CH] = jnp.concatenate(
            [o[p * WINDOW:(p + 1) * WINDOW] for p in range(GROUP)], axis=-1).astype(jnp.bfloat16)

    f_ref[...] = jnp.dot(mix_ref[...], wout_ref[...], preferred_element_type=jnp.float32)
    _residual_rows(x_ref, f_ref, gpost_ref, o_ref)


def _pair_perm():
    idx = []
    for p in range(GROUP):
        for half in range(N_KV_HEADS):
            head = p + GROUP * half
            idx.extend(range(head * HEAD_DIM, (head + 1) * HEAD_DIM))
    return np.array(idx, dtype=np.int32)


def _even_mixer(x, cs, g_pre, g_post, w_in, conv_w, conv_b, ln_g, ln_b, sinks, w_out):
    b, s, d = x.shape
    tm = SEQ_TILE
    nslab = A_CH // LANES
    perm = _pair_perm()
    wq = w_in[:, 2 * A_CH:2 * A_CH + Q_DIM][:, perm] * (HEAD_DIM ** -0.5)
    w_in_p = jnp.concatenate([w_in[:, :2 * A_CH], wq, w_in[:, 2 * A_CH + Q_DIM:]],
                             axis=1).astype(jnp.bfloat16)
    w_out_p = jnp.concatenate([w_out[:A_CH], w_out[A_CH:][perm]], axis=0).astype(jnp.bfloat16)
    cw = conv_w.reshape(A_CONV, nslab, 1, 1, LANES)
    cb = conv_b.reshape(nslab, 1, 1, LANES)
    sink_rows = jnp.broadcast_to(
        jnp.repeat(sinks.reshape(N_KV_HEADS, GROUP), WINDOW, axis=1)[..., None],
        (N_KV_HEADS, GROUP * WINDOW, LANES))
    e = jnp.asarray(_rope_expand_matrix(), jnp.bfloat16)
    cs_spec = pl.BlockSpec((None, tm, ROPE_DIM), lambda bb, i: (bb, i, 0))
    resident = (4 * tm * d * 4 + 2 * tm * LANES * 4 + w_in_p.size * 2 + w_out_p.size * 2
                + nslab * (tm + A_HALO) * LANES * 4 + tm * d * 6 + tm * Q_DIM * 2
                + N_KV_HEADS * (tm + WINDOW) * 3 * LANES * 2 + tm * d * 2)
    return pl.pallas_call(
        functools.partial(_even_kernel, tm=tm),
        grid=(b, s // tm),
        in_specs=[
            _x_spec(tm, d), cs_spec,
            _const_spec(e.shape),
            _const_spec((1, d)),
            _const_spec((1, d)),
            _const_spec(w_in_p.shape),
            _const_spec(cw.shape),
            _const_spec(cb.shape),
            _const_spec((1, A_CH)),
            _const_spec((1, A_CH)),
            _const_spec(sink_rows.shape),
            _const_spec(w_out_p.shape),
        ],
        out_specs=_x_spec(tm, d),
        out_shape=jax.ShapeDtypeStruct(x.shape, x.dtype),
        scratch_shapes=[
            pltpu.VMEM((nslab, 1, tm + A_HALO, LANES), jnp.float32),
            pltpu.VMEM((tm, d), jnp.bfloat16),
            pltpu.VMEM((tm, Q_DIM), jnp.bfloat16),
            pltpu.VMEM((N_KV_HEADS, tm + WINDOW, LANES), jnp.bfloat16),
            pltpu.VMEM((N_KV_HEADS, tm + WINDOW, 2 * LANES), jnp.bfloat16),
            pltpu.VMEM((tm, 2 * A_CH), jnp.bfloat16),
            pltpu.VMEM((tm, d), jnp.float32),
        ],
        compiler_params=pltpu.CompilerParams(
            dimension_semantics=("arbitrary", "arbitrary"),
            vmem_limit_bytes=_vmem_limit(resident)),
        name="even_mixer",
    )(x, cs, e, g_pre.reshape(1, d), g_post.reshape(1, d), w_in_p, cw, cb,
      ln_g.reshape(1, A_CH), ln_b.reshape(1, A_CH), sink_rows, w_out_p)


def kernel(x, positions, mix_norm_pre, mix_norm_post, ffn_norm_pre, ffn_norm_post, ev_w_in, ev_a_conv_w, ev_a_conv_b, ev_a_ln_g, ev_a_ln_b, ev_sinks, ev_w_out, od_w_in, od_conv_w, od_w_out, ffn_w_up, ffn_conv_w, ffn_w_down):
    depth = mix_norm_pre.shape[0]
    assert x.shape[1] % SEQ_TILE == 0 and SEQ_TILE % WINDOW == 0
    cs = _rope_tables(positions)
    bf = jnp.bfloat16
    for i in range(depth):
        j = i // 2
        if i % 2 == 0:
            x = _even_mixer(x, cs, mix_norm_pre[i], mix_norm_post[i], ev_w_in[j],
                            ev_a_conv_w[j], ev_a_conv_b[j], ev_a_ln_g[j], ev_a_ln_b[j],
                            ev_sinks[j], ev_w_out[j])
        else:
            x = _odd_mixer(x, mix_norm_pre[i], mix_norm_post[i], od_w_in[j].astype(bf),
                           od_conv_w[j], od_w_out[j].astype(bf))
        x = _ffn(x, ffn_norm_pre[i], ffn_norm_post[i], ffn_w_up[i].astype(bf),
                 ffn_conv_w[i], ffn_w_down[i].astype(bf))
    return x
```

```python
import functools

import jax
import jax.numpy as jnp
import numpy as np
from jax import lax
from jax.experimental import pallas as pl
from jax.experimental.pallas import tpu as pltpu

D_MODEL = 1024
HEAD_DIM = 64
A_CH = 512
A_CONV = 31
N_Q_HEADS = 8
N_KV_HEADS = 2
GROUP = 4
WINDOW = 128
ROPE_THETA = 500000.0
ROPE_DIM = 16
ROPE_HALF = ROPE_DIM // 2
Q_DIM = 512
KV_DIM = 128
EVEN_IN = 1792
SC_DIM = 1024
D_FF = 2816
RMS_EPS = 1e-6
LN_EPS = 1e-5

LANES = 128
SUBLANES = 8
VMEM_BYTES = 64 << 20

SEQ_TILE = 512
EVEN_TILE = 1024
FF_CHUNK = 256
CONV_ROWS = 64
ROW_BLK = 64
A_HALO = 32


def _rows(ref, start, size):
    return ref[:, pl.ds(0, 1, stride=2), pl.ds(start, size), :]


def _rms(x, g):
    ms = jnp.mean(x * x, axis=-1, keepdims=True)
    return x * lax.rsqrt(ms + RMS_EPS) * g


def _to_slabs(ref, row0, val):
    rows = val.shape[0]
    for k in range(val.shape[1] // LANES):
        ref[k, 0, row0:row0 + rows, :] = val[:, k * LANES:(k + 1) * LANES]


def _from_slabs(val):
    return jnp.concatenate([val[k, 0] for k in range(val.shape[0])], axis=-1)


def _conv3(ubuf, r0, rows, w):
    u0 = ubuf[:, :, SUBLANES + r0:SUBLANES + r0 + rows, :]
    u1 = _rows(ubuf, SUBLANES - 1 + r0, rows)
    u2 = _rows(ubuf, SUBLANES - 2 + r0, rows)
    return w[2] * u0 + w[1] * u1 + w[0] * u2


def _prenorm_rows(x_ref, g_ref, h_ref, r0, rows):
    g = g_ref[...]
    for r in range(r0, r0 + rows, ROW_BLK):
        h_ref[r:r + ROW_BLK, :] = _rms(x_ref[r:r + ROW_BLK, :], g).astype(jnp.bfloat16)


def _residual_rows(x_ref, f_ref, g_ref, o_ref, r0, rows):
    g = g_ref[...]
    for r in range(r0, r0 + rows, ROW_BLK):
        o_ref[r:r + ROW_BLK, :] = x_ref[r:r + ROW_BLK, :] + _rms(f_ref[r:r + ROW_BLK, :], g)


def _const_spec(shape):
    nd = len(shape)
    return pl.BlockSpec(shape, lambda b, i: (0,) * nd, pipeline_mode=pl.Buffered(1))


def _layer_spec(shape, layer):
    nd = len(shape)
    return pl.BlockSpec((None,) + tuple(shape[1:]), lambda b, i: (layer,) + (0,) * (nd - 1),
                        pipeline_mode=pl.Buffered(1))


def _x_spec(tm, width):
    return pl.BlockSpec((None, tm, width), lambda b, i: (b, i, 0))


def _vmem_limit(resident_bytes):
    return int(min(VMEM_BYTES - (4 << 20), 2 * resident_bytes + (8 << 20)))


def _rope_kernel(pos_ref, invf_ref, cs_ref):
    ang = invf_ref[...] * pos_ref[...]
    cs_ref[0:ROPE_HALF, :] = jnp.cos(ang)
    cs_ref[ROPE_HALF:ROPE_DIM, :] = jnp.sin(ang)


def _rope_tables(positions):
    inv_freq = ROPE_THETA ** (-(jnp.arange(ROPE_HALF, dtype=jnp.float32) * 2.0 / ROPE_DIM))
    pos = positions.astype(jnp.float32).reshape(1, -1)
    return pl.pallas_call(
        _rope_kernel,
        out_shape=jax.ShapeDtypeStruct((ROPE_DIM, pos.shape[1]), jnp.float32),
        name="rope_tables",
    )(pos, inv_freq.reshape(ROPE_HALF, 1))


def _rope_expand_matrix():
    e = np.zeros((ROPE_DIM, 3 * LANES), np.float32)
    for lane in range(LANES):
        d = lane % HEAD_DIM
        if d < ROPE_HALF:
            e[d, lane] = 1.0
            e[ROPE_HALF + d, LANES + lane] = -1.0
        elif d < ROPE_DIM:
            e[d - ROPE_HALF, lane] = 1.0
            e[d, 2 * LANES + lane] = 1.0
    return e


def _ffn_kernel(x_ref, gpre_ref, gpost_ref, wup_ref, cw_ref, wdown_ref, o_ref,
                ubuf_a, ubuf_b, carry, h_ref, act_ref, f_ref, *, tm):
    i = pl.program_id(1)
    nk = FF_CHUNK // LANES
    nslab_half = D_FF // LANES
    nch = D_FF // FF_CHUNK

    @pl.when(i == 0)
    def _():
        carry[...] = jnp.zeros_like(carry)

    def up(c, ub):
        ug = jnp.dot(h_ref[...], wup_ref[c], preferred_element_type=jnp.float32)
        uv = jnp.dot(h_ref[...], wup_ref[nch + c], preferred_element_type=jnp.float32)
        sg = c * nk
        sv = nslab_half + c * nk
        ub[0:nk, :, 0:SUBLANES, :] = carry[sg:sg + nk]
        ub[nk:2 * nk, :, 0:SUBLANES, :] = carry[sv:sv + nk]
        _to_slabs(ub.at[0:nk], SUBLANES, ug)
        _to_slabs(ub.at[nk:2 * nk], SUBLANES, uv)
        carry[sg:sg + nk] = ub[0:nk, :, tm:tm + SUBLANES, :]
        carry[sv:sv + nk] = ub[nk:2 * nk, :, tm:tm + SUBLANES, :]

    def act(c, ub):
        w = jnp.concatenate([cw_ref[:, c * nk:(c + 1) * nk],
                             cw_ref[:, nslab_half + c * nk:nslab_half + (c + 1) * nk]], axis=1)
        for r0 in range(0, tm, ROW_BLK):
            y = _conv3(ub, r0, ROW_BLK, w)
            yg = y[0:nk]
            a = (yg * jax.nn.sigmoid(yg) * y[nk:2 * nk]).astype(jnp.bfloat16)
            act_ref[r0:r0 + ROW_BLK, c * FF_CHUNK:(c + 1) * FF_CHUNK] = _from_slabs(a)

    _prenorm_rows(x_ref, gpre_ref, h_ref, 0, tm)
    for c in range(nch):
        ub = ubuf_a if c % 2 == 0 else ubuf_b
        up(c, ub)
        act(c, ub)
    f_ref[...] = jnp.dot(act_ref[...], wdown_ref[...], preferred_element_type=jnp.float32)
    _residual_rows(x_ref, f_ref, gpost_ref, o_ref, 0, tm)


def _ffn(x, layer, g_pre, g_post, w_up_c, conv_w, w_down):
    b, s, d = x.shape
    tm = SEQ_TILE
    nslab = 2 * D_FF // LANES
    nk = FF_CHUNK // LANES
    cw = conv_w.reshape(3, nslab, 1, 1, LANES)
    resident = (4 * tm * d * 4 + 2 * d * D_FF * 2 + D_FF * d * 2 + tm * D_FF * 2
                + tm * d * 6 + 4 * nk * (tm + SUBLANES) * LANES * 4)
    slab_buf = pltpu.VMEM((2 * nk, 1, tm + SUBLANES, LANES), jnp.float32)
    return pl.pallas_call(
        functools.partial(_ffn_kernel, tm=tm),
        grid=(b, s // tm),
        in_specs=[
            _x_spec(tm, d),
            _const_spec((1, d)),
            _const_spec((1, d)),
            _layer_spec(w_up_c.shape, layer),
            _const_spec(cw.shape),
            _layer_spec(w_down.shape, layer),
        ],
        out_specs=_x_spec(tm, d),
        out_shape=jax.ShapeDtypeStruct(x.shape, x.dtype),
        scratch_shapes=[
            slab_buf,
            slab_buf,
            pltpu.VMEM((nslab, 1, SUBLANES, LANES), jnp.float32),
            pltpu.VMEM((tm, d), jnp.bfloat16),
            pltpu.VMEM((tm, D_FF), jnp.bfloat16),
            pltpu.VMEM((tm, d), jnp.float32),
        ],
        compiler_params=pltpu.CompilerParams(
            dimension_semantics=("arbitrary", "arbitrary"),
            vmem_limit_bytes=_vmem_limit(resident)),
        name="ffn",
    )(x, g_pre.reshape(1, d), g_post.reshape(1, d), w_up_c, cw, w_down)


def _odd_kernel(x_ref, gpre_ref, gpost_ref, win_ref, cw_ref, wout_ref, o_ref,
                ubuf, carry, h_ref, y_ref, f_ref, *, tm):
    i = pl.program_id(1)
    nk = FF_CHUNK // LANES

    @pl.when(i == 0)
    def _():
        carry[...] = jnp.zeros_like(carry)

    _prenorm_rows(x_ref, gpre_ref, h_ref, 0, tm)
    for c in range(SC_DIM // FF_CHUNK):
        col = c * FF_CHUNK
        zb = jnp.dot(h_ref[...], win_ref[:, col:col + FF_CHUNK],
                     preferred_element_type=jnp.float32)
        zc = jnp.dot(h_ref[...], win_ref[:, SC_DIM + col:SC_DIM + col + FF_CHUNK],
                     preferred_element_type=jnp.float32)
        zu = jnp.dot(h_ref[...], win_ref[:, 2 * SC_DIM + col:2 * SC_DIM + col + FF_CHUNK],
                     preferred_element_type=jnp.float32)
        s0 = c * nk
        ub = ubuf.at[c % 2]
        ub[0:nk, :, 0:SUBLANES, :] = carry[s0:s0 + nk]
        _to_slabs(ub.at[0:nk], SUBLANES, zc * zu)
        _to_slabs(ub.at[nk:2 * nk], SUBLANES, zb)
        carry[s0:s0 + nk] = ub[0:nk, :, tm:tm + SUBLANES, :]
        w = cw_ref[:, s0:s0 + nk]
        for r0 in range(0, tm, ROW_BLK):
            y = _conv3(ub.at[0:nk], r0, ROW_BLK, w)
            b_gate = ub[nk:2 * nk, :, SUBLANES + r0:SUBLANES + r0 + ROW_BLK, :]
            y_ref[r0:r0 + ROW_BLK, col:col + FF_CHUNK] = _from_slabs(
                (b_gate * y).astype(jnp.bfloat16))
    f_ref[...] = jnp.dot(y_ref[...], wout_ref[...], preferred_element_type=jnp.float32)
    _residual_rows(x_ref, f_ref, gpost_ref, o_ref, 0, tm)


def _odd_mixer(x, g_pre, g_post, w_in, conv_w, w_out):
    b, s, d = x.shape
    tm = SEQ_TILE
    nslab = SC_DIM // LANES
    nk = FF_CHUNK // LANES
    cw = conv_w.reshape(3, nslab, 1, 1, LANES)
    resident = (4 * tm * d * 4 + w_in.size * 2 + w_out.size * 2 + tm * SC_DIM * 2
                + tm * d * 6 + 4 * nk * (tm + SUBLANES) * LANES * 4)
    return pl.pallas_call(
        functools.partial(_odd_kernel, tm=tm),
        grid=(b, s // tm),
        in_specs=[
            _x_spec(tm, d),
            _const_spec((1, d)),
            _const_spec((1, d)),
            _const_spec(w_in.shape),
            _const_spec(cw.shape),
            _const_spec(w_out.shape),
        ],
        out_specs=_x_spec(tm, d),
        out_shape=jax.ShapeDtypeStruct(x.shape, x.dtype),
        scratch_shapes=[
            pltpu.VMEM((2, 2 * nk, 1, tm + SUBLANES, LANES), jnp.float32),
            pltpu.VMEM((nslab, 1, SUBLANES, LANES), jnp.float32),
            pltpu.VMEM((tm, d), jnp.bfloat16),
            pltpu.VMEM((tm, SC_DIM), jnp.bfloat16),
            pltpu.VMEM((tm, d), jnp.float32),
        ],
        compiler_params=pltpu.CompilerParams(
            dimension_semantics=("arbitrary", "arbitrary"),
            vmem_limit_bytes=_vmem_limit(resident)),
        name="odd_mixer",
    )(x, g_pre.reshape(1, d), g_post.reshape(1, d), w_in, cw, w_out)


def _split3_bf16(x):
    hi = x.astype(jnp.bfloat16)
    r = x - hi.astype(jnp.float32)
    mid = r.astype(jnp.bfloat16)
    lo = (r - mid.astype(jnp.float32)).astype(jnp.bfloat16)
    return hi, mid, lo


def _rope(x, c, s_up, s_dn):
    up = pltpu.roll(x, LANES - ROPE_HALF, axis=1)
    dn = pltpu.roll(x, ROPE_HALF, axis=1)
    return x * c + up * s_up + dn * s_dn


def _band_bias():
    r = np.arange(GROUP * WINDOW)[:, None] % WINDOW
    col = np.arange(2 * WINDOW)[None, :]
    band = (col > r) & (col <= r + WINDOW)
    first = band & (col >= WINDOW)
    return np.where(np.stack([band, first]), 0.0, -np.inf).astype(np.float32)


def _even_kernel(x_ref, cs_ref, e_ref, bias_ref, gpre_ref, gpost_ref, win_ref, cw_ref, cb_ref,
                 lng_ref, lnb_ref, sink_ref, wout_ref, o_ref,
                 abuf, h_ref, q_ref, kg_ref, vg_ref, mix_ref, f_ref, *, tm):
    i = pl.program_id(1)
    nslab = A_CH // LANES

    @pl.when(i == 0)
    def _():
        abuf[:, :, 0:A_HALO, :] = jnp.zeros((nslab, 1, A_HALO, LANES), jnp.float32)
        kg_ref[:, 0:WINDOW, :] = jnp.zeros((N_KV_HEADS, WINDOW, LANES), jnp.bfloat16)
        vg_ref[:, 0:WINDOW, 0:LANES] = jnp.zeros((N_KV_HEADS, WINDOW, LANES), jnp.bfloat16)
        vg_ref[:, :, LANES:2 * LANES] = jnp.ones((N_KV_HEADS, tm + WINDOW, LANES), jnp.bfloat16)

    @pl.when(i > 0)
    def _():
        abuf[:, :, 0:A_HALO, :] = abuf[:, :, tm:tm + A_HALO, :]
        kg_ref[:, 0:WINDOW, :] = kg_ref[:, tm:tm + WINDOW, :]
        vg_ref[:, 0:WINDOW, 0:LANES] = vg_ref[:, tm:tm + WINDOW, 0:LANES]

    def attend(j, g):
        qs = jnp.concatenate(
            [q_ref[j * WINDOW:(j + 1) * WINDOW, p * LANES:(p + 1) * LANES] for p in range(GROUP)],
            axis=0)
        kw = kg_ref[g, j * WINDOW:(j + 2) * WINDOW, :]
        vw = vg_ref[g, j * WINDOW:(j + 2) * WINDOW, :]
        sc = lax.dot_general(qs, kw, (((1,), (1,)), ((), ())),
                             preferred_element_type=jnp.float32)
        sc = sc + (bias_ref[jnp.where(i > 0, 0, 1)] if j == 0 else bias_ref[0])
        sink = sink_ref[g]
        m = jnp.maximum(jnp.max(sc, axis=-1, keepdims=True), sink)
        p_un = jnp.exp(sc - jnp.concatenate([m, m], axis=-1))
        pv = jnp.dot(p_un.astype(jnp.bfloat16), vw, preferred_element_type=jnp.float32)
        denom = pv[:, LANES:2 * LANES] + jnp.exp(sink - m)
        return pv[:, 0:LANES] * (1.0 / denom)

    def conv_ln(r0):
        acc = jnp.broadcast_to(cb_ref[...], (nslab, 1, CONV_ROWS, LANES))
        for tap in range(A_CONV):
            acc = acc + _rows(abuf, r0 + (A_HALO - A_CONV + 1) + tap, CONV_ROWS) * cw_ref[tap]
        y = _from_slabs(acc)
        mu = jnp.mean(y, axis=-1, keepdims=True)
        yc = y - mu
        y = yc * lax.rsqrt(jnp.mean(yc * yc, axis=-1, keepdims=True) + LN_EPS)
        y = y * lng_ref[...] + lnb_ref[...]
        mix_ref[r0:r0 + CONV_ROWS, 0:A_CH] = (y * jax.nn.sigmoid(y)).astype(jnp.bfloat16)

    rows = WINDOW

    def glu_in(j):
        r0 = j * rows
        _prenorm_rows(x_ref, gpre_ref, h_ref, r0, rows)
        za = jnp.dot(h_ref[r0:r0 + rows, :], win_ref[:, 0:2 * A_CH],
                     preferred_element_type=jnp.float32)
        _to_slabs(abuf, A_HALO + r0, za[:, 0:A_CH] * jax.nn.sigmoid(za[:, A_CH:2 * A_CH]))

    def qkv_in(j):
        r0 = j * rows
        t = None
        for piece in _split3_bf16(cs_ref[:, r0:r0 + rows]):
            d = lax.dot_general(piece, e_ref[...], (((0,), (0,)), ((), ())),
                                preferred_element_type=jnp.float32)
            t = d if t is None else t + d
        lane = lax.broadcasted_iota(jnp.int32, (rows, LANES), 1)
        c = t[:, 0:LANES] + jnp.where((lane & (HEAD_DIM - 1)) >= ROPE_DIM, 1.0, 0.0)
        s_up = t[:, LANES:2 * LANES]
        s_dn = t[:, 2 * LANES:3 * LANES]

        zq = jnp.dot(h_ref[r0:r0 + rows, :], win_ref[:, 2 * A_CH:EVEN_IN],
                     preferred_element_type=jnp.float32)
        for p in range(GROUP):
            q_ref[r0:r0 + rows, p * LANES:(p + 1) * LANES] = _rope(
                zq[:, p * LANES:(p + 1) * LANES], c, s_up, s_dn).astype(jnp.bfloat16)
        k = _rope(zq[:, Q_DIM:Q_DIM + KV_DIM], c, s_up, s_dn).astype(jnp.bfloat16)
        v = zq[:, Q_DIM + KV_DIM:Q_DIM + 2 * KV_DIM].astype(jnp.bfloat16)
        zero = jnp.zeros_like(k)
        for g in range(N_KV_HEADS):
            mine = (lane >= g * HEAD_DIM) & (lane < (g + 1) * HEAD_DIM)
            kg_ref[g, WINDOW + r0:WINDOW + r0 + rows, :] = jnp.where(mine, k, zero)
            vg_ref[g, WINDOW + r0:WINDOW + r0 + rows, 0:LANES] = jnp.where(mine, v, zero)

    def out(j):
        r0 = j * rows
        f_ref[r0:r0 + rows, :] = jnp.dot(mix_ref[r0:r0 + rows, :], wout_ref[...],
                                         preferred_element_type=jnp.float32)
        _residual_rows(x_ref, f_ref, gpost_ref, o_ref, r0, rows)

    nblk = tm // rows
    conv_per_blk = rows // CONV_ROWS
    assert conv_per_blk == N_KV_HEADS
    for j in range(min(2, nblk)):
        glu_in(j)
        qkv_in(j)
    for j in range(nblk):
        o = attend(j, 0)
        if j + 2 < nblk:
            glu_in(j + 2)
        conv_ln(j * rows)
        o = o + attend(j, 1)
        if j + 2 < nblk:
            qkv_in(j + 2)
        conv_ln(j * rows + CONV_ROWS)
        mix_ref[j * rows:(j + 1) * rows, A_CH:2 * A_CH] = jnp.concatenate(
            [o[p * WINDOW:(p + 1) * WINDOW] for p in range(GROUP)], axis=-1).astype(jnp.bfloat16)
        out(j)


def _pair_perm():
    idx = []
    for p in range(GROUP):
        for half in range(N_KV_HEADS):
            head = p + GROUP * half
            idx.extend(range(head * HEAD_DIM, (head + 1) * HEAD_DIM))
    return np.array(idx, dtype=np.int32)


def _even_mixer(x, cs, g_pre, g_post, w_in, conv_w, conv_b, ln_g, ln_b, sinks, w_out):
    b, s, d = x.shape
    tm = EVEN_TILE
    nslab = A_CH // LANES
    perm = _pair_perm()
    wq = w_in[:, 2 * A_CH:2 * A_CH + Q_DIM][:, perm] * (HEAD_DIM ** -0.5)
    w_in_p = jnp.concatenate([w_in[:, :2 * A_CH], wq, w_in[:, 2 * A_CH + Q_DIM:]],
                             axis=1).astype(jnp.bfloat16)
    w_out_p = jnp.concatenate([w_out[:A_CH], w_out[A_CH:][perm]], axis=0).astype(jnp.bfloat16)
    cw = conv_w.reshape(A_CONV, nslab, 1, 1, LANES)
    cb = conv_b.reshape(nslab, 1, 1, LANES)
    sink_rows = jnp.broadcast_to(
        jnp.repeat(sinks.reshape(N_KV_HEADS, GROUP), WINDOW, axis=1)[..., None],
        (N_KV_HEADS, GROUP * WINDOW, LANES))
    e = jnp.asarray(_rope_expand_matrix(), jnp.bfloat16)
    bias = jnp.asarray(_band_bias())
    steps = s // tm
    cs_spec = pl.BlockSpec((ROPE_DIM, tm), lambda bb, i: (0, bb * steps + i))
    resident = (4 * tm * d * 4 + 2 * ROPE_DIM * tm * 4 + w_in_p.size * 2 + w_out_p.size * 2
                + bias.size * 4 + nslab * (tm + A_HALO) * LANES * 4 + tm * d * 6 + tm * Q_DIM * 2
                + N_KV_HEADS * (tm + WINDOW) * 3 * LANES * 2 + tm * d * 2)
    return pl.pallas_call(
        functools.partial(_even_kernel, tm=tm),
        grid=(b, steps),
        in_specs=[
            _x_spec(tm, d), cs_spec,
            _const_spec(e.shape),
            _const_spec(bias.shape),
            _const_spec((1, d)),
            _const_spec((1, d)),
            _const_spec(w_in_p.shape),
            _const_spec(cw.shape),
            _const_spec(cb.shape),
            _const_spec((1, A_CH)),
            _const_spec((1, A_CH)),
            _const_spec(sink_rows.shape),
            _const_spec(w_out_p.shape),
        ],
        out_specs=_x_spec(tm, d),
        out_shape=jax.ShapeDtypeStruct(x.shape, x.dtype),
        scratch_shapes=[
            pltpu.VMEM((nslab, 1, tm + A_HALO, LANES), jnp.float32),
            pltpu.VMEM((tm, d), jnp.bfloat16),
            pltpu.VMEM((tm, Q_DIM), jnp.bfloat16),
            pltpu.VMEM((N_KV_HEADS, tm + WINDOW, LANES), jnp.bfloat16),
            pltpu.VMEM((N_KV_HEADS, tm + WINDOW, 2 * LANES), jnp.bfloat16),
            pltpu.VMEM((tm, 2 * A_CH), jnp.bfloat16),
            pltpu.VMEM((tm, d), jnp.float32),
        ],
        compiler_params=pltpu.CompilerParams(
            dimension_semantics=("arbitrary", "arbitrary"),
            vmem_limit_bytes=_vmem_limit(resident)),
        name="even_mixer",
    )(x, cs, e, bias, g_pre.reshape(1, d), g_post.reshape(1, d), w_in_p, cw, cb,
      ln_g.reshape(1, A_CH), ln_b.reshape(1, A_CH), sink_rows, w_out_p)


def kernel(x, positions, mix_norm_pre, mix_norm_post, ffn_norm_pre, ffn_norm_post, ev_w_in, ev_a_conv_w, ev_a_conv_b, ev_a_ln_g, ev_a_ln_b, ev_sinks, ev_w_out, od_w_in, od_conv_w, od_w_out, ffn_w_up, ffn_conv_w, ffn_w_down):
    depth = mix_norm_pre.shape[0]
    d = x.shape[2]
    assert x.shape[1] % SEQ_TILE == 0 and x.shape[1] % EVEN_TILE == 0 and EVEN_TILE % WINDOW == 0
    cs = _rope_tables(positions)
    bf = jnp.bfloat16
    nch = D_FF // FF_CHUNK
    w_up_c = ffn_w_up.reshape(depth, d, 2 * nch, FF_CHUNK).transpose(0, 2, 1, 3).astype(bf)
    w_down = ffn_w_down.astype(bf)
    for i in range(depth):
        j = i // 2
        if i % 2 == 0:
            x = _even_mixer(x, cs, mix_norm_pre[i], mix_norm_post[i], ev_w_in[j],
                            ev_a_conv_w[j], ev_a_conv_b[j], ev_a_ln_g[j], ev_a_ln_b[j],
                            ev_sinks[j], ev_w_out[j])
        else:
            x = _odd_mixer(x, mix_norm_pre[i], mix_norm_post[i], od_w_in[j].astype(bf),
                           od_conv_w[j], od_w_out[j].astype(bf))
        x = _ffn(x, i, ffn_norm_pre[i], ffn_norm_post[i], w_up_c, ffn_conv_w[i], w_down)
    return x
```

```python
import functools

import jax
import jax.numpy as jnp
import numpy as np
from jax import lax
from jax.experimental import pallas as pl
from jax.experimental.pallas import tpu as pltpu

D_MODEL = 1024
HEAD_DIM = 64
A_CH = 512
A_CONV = 31
N_Q_HEADS = 8
N_KV_HEADS = 2
GROUP = 4
WINDOW = 128
ROPE_THETA = 500000.0
ROPE_DIM = 16
ROPE_HALF = ROPE_DIM // 2
Q_DIM = 512
KV_DIM = 128
EVEN_IN = 1792
SC_DIM = 1024
D_FF = 2816
RMS_EPS = 1e-6
LN_EPS = 1e-5

LANES = 128
SUBLANES = 8
VMEM_BYTES = 64 << 20

SEQ_TILE = 512
EVEN_TILE = 512
FF_CHUNK = 256
CONV_ROWS = 32
ROW_BLK = 64
A_HALO = 32


def _rows(ref, start, size):
    return ref[:, pl.ds(0, 1, stride=2), pl.ds(start, size), :]


def _rms(x, g):
    ms = jnp.mean(x * x, axis=-1, keepdims=True)
    return x * lax.rsqrt(ms + RMS_EPS) * g


def _to_slabs(ref, row0, val):
    rows = val.shape[0]
    for k in range(val.shape[1] // LANES):
        ref[k, 0, row0:row0 + rows, :] = val[:, k * LANES:(k + 1) * LANES]


def _from_slabs(val):
    return jnp.concatenate([val[k, 0] for k in range(val.shape[0])], axis=-1)


def _conv3(ubuf, r0, rows, w):
    u0 = ubuf[:, :, SUBLANES + r0:SUBLANES + r0 + rows, :]
    u1 = _rows(ubuf, SUBLANES - 1 + r0, rows)
    u2 = _rows(ubuf, SUBLANES - 2 + r0, rows)
    return w[2] * u0 + w[1] * u1 + w[0] * u2


def _prenorm_rows(x_ref, g_ref, h_ref, r0, rows):
    g = g_ref[...]
    for r in range(r0, r0 + rows, ROW_BLK):
        h_ref[r:r + ROW_BLK, :] = _rms(x_ref[r:r + ROW_BLK, :], g).astype(jnp.bfloat16)


def _residual_rows(x_ref, f_ref, g_ref, o_ref, r0, rows):
    g = g_ref[...]
    for r in range(r0, r0 + rows, ROW_BLK):
        o_ref[r:r + ROW_BLK, :] = x_ref[r:r + ROW_BLK, :] + _rms(f_ref[r:r + ROW_BLK, :], g)


def _const_spec(shape):
    nd = len(shape)
    return pl.BlockSpec(shape, lambda b, i: (0,) * nd, pipeline_mode=pl.Buffered(1))


def _layer_spec(shape, layer):
    nd = len(shape)
    return pl.BlockSpec((None,) + tuple(shape[1:]), lambda b, i: (layer,) + (0,) * (nd - 1),
                        pipeline_mode=pl.Buffered(1))


def _x_spec(tm, width):
    return pl.BlockSpec((None, tm, width), lambda b, i: (b, i, 0))


def _vmem_limit(resident_bytes):
    return int(min(VMEM_BYTES - (4 << 20), 2 * resident_bytes + (8 << 20)))


def _rope_kernel(pos_ref, invf_ref, cs_ref):
    ang = invf_ref[...] * pos_ref[...]
    cs_ref[0:ROPE_HALF, :] = jnp.cos(ang)
    cs_ref[ROPE_HALF:ROPE_DIM, :] = jnp.sin(ang)


def _rope_tables(positions):
    inv_freq = ROPE_THETA ** (-(jnp.arange(ROPE_HALF, dtype=jnp.float32) * 2.0 / ROPE_DIM))
    pos = positions.astype(jnp.float32).reshape(1, -1)
    return pl.pallas_call(
        _rope_kernel,
        out_shape=jax.ShapeDtypeStruct((ROPE_DIM, pos.shape[1]), jnp.float32),
        name="rope_tables",
    )(pos, inv_freq.reshape(ROPE_HALF, 1))


def _rope_expand_matrix():
    e = np.zeros((ROPE_DIM, 3 * LANES), np.float32)
    for lane in range(LANES):
        d = lane % HEAD_DIM
        if d < ROPE_HALF:
            e[d, lane] = 1.0
            e[ROPE_HALF + d, LANES + lane] = -1.0
        elif d < ROPE_DIM:
            e[d - ROPE_HALF, lane] = 1.0
            e[d, 2 * LANES + lane] = 1.0
    return e


def _ffn_kernel(x_ref, gpre_ref, gpost_ref, wup_ref, cw_ref, wdown_ref, o_ref,
                ubuf_a, ubuf_b, carry, h_ref, act_ref, f_ref, *, tm):
    i = pl.program_id(1)
    nk = FF_CHUNK // LANES
    nslab_half = D_FF // LANES
    nch = D_FF // FF_CHUNK

    @pl.when(i == 0)
    def _():
        carry[...] = jnp.zeros_like(carry)

    def up(c, ub):
        ug = jnp.dot(h_ref[...], wup_ref[c], preferred_element_type=jnp.float32)
        uv = jnp.dot(h_ref[...], wup_ref[nch + c], preferred_element_type=jnp.float32)
        sg = c * nk
        sv = nslab_half + c * nk
        ub[0:nk, :, 0:SUBLANES, :] = carry[sg:sg + nk]
        ub[nk:2 * nk, :, 0:SUBLANES, :] = carry[sv:sv + nk]
        _to_slabs(ub.at[0:nk], SUBLANES, ug)
        _to_slabs(ub.at[nk:2 * nk], SUBLANES, uv)
        carry[sg:sg + nk] = ub[0:nk, :, tm:tm + SUBLANES, :]
        carry[sv:sv + nk] = ub[nk:2 * nk, :, tm:tm + SUBLANES, :]

    def act(c, ub):
        w = jnp.concatenate([cw_ref[:, c * nk:(c + 1) * nk],
                             cw_ref[:, nslab_half + c * nk:nslab_half + (c + 1) * nk]], axis=1)
        for r0 in range(0, tm, ROW_BLK):
            y = _conv3(ub, r0, ROW_BLK, w)
            yg = y[0:nk]
            a = (yg * jax.nn.sigmoid(yg) * y[nk:2 * nk]).astype(jnp.bfloat16)
            act_ref[r0:r0 + ROW_BLK, c * FF_CHUNK:(c + 1) * FF_CHUNK] = _from_slabs(a)

    def down(c):
        part = jnp.dot(act_ref[:, c * FF_CHUNK:(c + 1) * FF_CHUNK],
                       wdown_ref[c * FF_CHUNK:(c + 1) * FF_CHUNK, :],
                       preferred_element_type=jnp.float32)
        if c == 0:
            f_ref[...] = part
        else:
            f_ref[...] += part

    bufs = (ubuf_a, ubuf_b)
    _prenorm_rows(x_ref, gpre_ref, h_ref, 0, tm)
    up(0, bufs[0])
    for c in range(nch):
        if c + 1 < nch:
            up(c + 1, bufs[(c + 1) % 2])
        act(c, bufs[c % 2])
        down(c)
    _residual_rows(x_ref, f_ref, gpost_ref, o_ref, 0, tm)


def _ffn(x, layer, g_pre, g_post, w_up_c, conv_w, w_down):
    b, s, d = x.shape
    tm = SEQ_TILE
    nslab = 2 * D_FF // LANES
    nk = FF_CHUNK // LANES
    cw = conv_w.reshape(3, nslab, 1, 1, LANES)
    resident = (4 * tm * d * 4 + 2 * d * D_FF * 2 + D_FF * d * 2 + tm * D_FF * 2
                + tm * d * 6 + 4 * nk * (tm + SUBLANES) * LANES * 4)
    slab_buf = pltpu.VMEM((2 * nk, 1, tm + SUBLANES, LANES), jnp.float32)
    return pl.pallas_call(
        functools.partial(_ffn_kernel, tm=tm),
        grid=(b, s // tm),
        in_specs=[
            _x_spec(tm, d),
            _const_spec((1, d)),
            _const_spec((1, d)),
            _layer_spec(w_up_c.shape, layer),
            _const_spec(cw.shape),
            _layer_spec(w_down.shape, layer),
        ],
        out_specs=_x_spec(tm, d),
        out_shape=jax.ShapeDtypeStruct(x.shape, x.dtype),
        scratch_shapes=[
            slab_buf,
            slab_buf,
            pltpu.VMEM((nslab, 1, SUBLANES, LANES), jnp.float32),
            pltpu.VMEM((tm, d), jnp.bfloat16),
            pltpu.VMEM((tm, D_FF), jnp.bfloat16),
            pltpu.VMEM((tm, d), jnp.float32),
        ],
        compiler_params=pltpu.CompilerParams(
            dimension_semantics=("arbitrary", "arbitrary"),
            vmem_limit_bytes=_vmem_limit(resident)),
        name="ffn",
    )(x, g_pre.reshape(1, d), g_post.reshape(1, d), w_up_c, cw, w_down)


def _odd_kernel(x_ref, gpre_ref, gpost_ref, win_ref, cw_ref, wout_ref, o_ref,
                ubuf, carry, h_ref, y_ref, f_ref, *, tm):
    i = pl.program_id(1)
    nk = FF_CHUNK // LANES

    @pl.when(i == 0)
    def _():
        carry[...] = jnp.zeros_like(carry)

    _prenorm_rows(x_ref, gpre_ref, h_ref, 0, tm)
    for c in range(SC_DIM // FF_CHUNK):
        col = c * FF_CHUNK
        zb = jnp.dot(h_ref[...], win_ref[:, col:col + FF_CHUNK],
                     preferred_element_type=jnp.float32)
        zc = jnp.dot(h_ref[...], win_ref[:, SC_DIM + col:SC_DIM + col + FF_CHUNK],
                     preferred_element_type=jnp.float32)
        zu = jnp.dot(h_ref[...], win_ref[:, 2 * SC_DIM + col:2 * SC_DIM + col + FF_CHUNK],
                     preferred_element_type=jnp.float32)
        s0 = c * nk
        ub = ubuf.at[c % 2]
        ub[0:nk, :, 0:SUBLANES, :] = carry[s0:s0 + nk]
        _to_slabs(ub.at[0:nk], SUBLANES, zc * zu)
        _to_slabs(ub.at[nk:2 * nk], SUBLANES, zb)
        carry[s0:s0 + nk] = ub[0:nk, :, tm:tm + SUBLANES, :]
        w = cw_ref[:, s0:s0 + nk]
        for r0 in range(0, tm, ROW_BLK):
            y = _conv3(ub.at[0:nk], r0, ROW_BLK, w)
            b_gate = ub[nk:2 * nk, :, SUBLANES + r0:SUBLANES + r0 + ROW_BLK, :]
            y_ref[r0:r0 + ROW_BLK, col:col + FF_CHUNK] = _from_slabs(
                (b_gate * y).astype(jnp.bfloat16))
    f_ref[...] = jnp.dot(y_ref[...], wout_ref[...], preferred_element_type=jnp.float32)
    _residual_rows(x_ref, f_ref, gpost_ref, o_ref, 0, tm)


def _odd_mixer(x, g_pre, g_post, w_in, conv_w, w_out):
    b, s, d = x.shape
    tm = SEQ_TILE
    nslab = SC_DIM // LANES
    nk = FF_CHUNK // LANES
    cw = conv_w.reshape(3, nslab, 1, 1, LANES)
    resident = (4 * tm * d * 4 + w_in.size * 2 + w_out.size * 2 + tm * SC_DIM * 2
                + tm * d * 6 + 4 * nk * (tm + SUBLANES) * LANES * 4)
    return pl.pallas_call(
        functools.partial(_odd_kernel, tm=tm),
        grid=(b, s // tm),
        in_specs=[
            _x_spec(tm, d),
            _const_spec((1, d)),
            _const_spec((1, d)),
            _const_spec(w_in.shape),
            _const_spec(cw.shape),
            _const_spec(w_out.shape),
        ],
        out_specs=_x_spec(tm, d),
        out_shape=jax.ShapeDtypeStruct(x.shape, x.dtype),
        scratch_shapes=[
            pltpu.VMEM((2, 2 * nk, 1, tm + SUBLANES, LANES), jnp.float32),
            pltpu.VMEM((nslab, 1, SUBLANES, LANES), jnp.float32),
            pltpu.VMEM((tm, d), jnp.bfloat16),
            pltpu.VMEM((tm, SC_DIM), jnp.bfloat16),
            pltpu.VMEM((tm, d), jnp.float32),
        ],
        compiler_params=pltpu.CompilerParams(
            dimension_semantics=("arbitrary", "arbitrary"),
            vmem_limit_bytes=_vmem_limit(resident)),
        name="odd_mixer",
    )(x, g_pre.reshape(1, d), g_post.reshape(1, d), w_in, cw, w_out)


def _split3_bf16(x):
    hi = x.astype(jnp.bfloat16)
    r = x - hi.astype(jnp.float32)
    mid = r.astype(jnp.bfloat16)
    lo = (r - mid.astype(jnp.float32)).astype(jnp.bfloat16)
    return hi, mid, lo


def _rope(x, c, s_up, s_dn):
    up = pltpu.roll(x, LANES - ROPE_HALF, axis=1)
    dn = pltpu.roll(x, ROPE_HALF, axis=1)
    return x * c + up * s_up + dn * s_dn


def _band_bias():
    r = np.arange(GROUP * WINDOW)[:, None] % WINDOW
    col = np.arange(2 * WINDOW)[None, :]
    band = (col > r) & (col <= r + WINDOW)
    first = band & (col >= WINDOW)
    return np.where(np.stack([band, first]), 0.0, -np.inf).astype(np.float32)


def _even_kernel(x_ref, cs_ref, e_ref, bias_ref, gpre_ref, gpost_ref, win_ref, cw_ref, cb_ref,
                 lng_ref, lnb_ref, sink_ref, wout_ref, o_ref,
                 abuf, h_ref, q_ref, kg_ref, vg_ref, mix_ref, f_ref, *, tm):
    i = pl.program_id(1)
    nslab = A_CH // LANES

    @pl.when(i == 0)
    def _():
        abuf[:, :, 0:A_HALO, :] = jnp.zeros((nslab, 1, A_HALO, LANES), jnp.float32)
        kg_ref[:, 0:WINDOW, :] = jnp.zeros((N_KV_HEADS, WINDOW, LANES), jnp.bfloat16)
        vg_ref[:, 0:WINDOW, 0:LANES] = jnp.zeros((N_KV_HEADS, WINDOW, LANES), jnp.bfloat16)
        vg_ref[:, :, LANES:2 * LANES] = jnp.ones((N_KV_HEADS, tm + WINDOW, LANES), jnp.bfloat16)

    @pl.when(i > 0)
    def _():
        abuf[:, :, 0:A_HALO, :] = abuf[:, :, tm:tm + A_HALO, :]
        kg_ref[:, 0:WINDOW, :] = kg_ref[:, tm:tm + WINDOW, :]
        vg_ref[:, 0:WINDOW, 0:LANES] = vg_ref[:, tm:tm + WINDOW, 0:LANES]

    def attend(j, g):
        qs = jnp.concatenate(
            [q_ref[j * WINDOW:(j + 1) * WINDOW, p * LANES:(p + 1) * LANES] for p in range(GROUP)],
            axis=0)
        kw = kg_ref[g, j * WINDOW:(j + 2) * WINDOW, :]
        vw = vg_ref[g, j * WINDOW:(j + 2) * WINDOW, :]
        sc = lax.dot_general(qs, kw, (((1,), (1,)), ((), ())),
                             preferred_element_type=jnp.float32)
        sc = sc + (bias_ref[jnp.where(i > 0, 0, 1)] if j == 0 else bias_ref[0])
        sink = sink_ref[g]
        m = jnp.maximum(jnp.max(sc, axis=-1, keepdims=True), sink)
        p_un = jnp.exp(sc - jnp.concatenate([m, m], axis=-1))
        pv = jnp.dot(p_un.astype(jnp.bfloat16), vw, preferred_element_type=jnp.float32)
        denom = pv[:, LANES:2 * LANES] + jnp.exp(sink - m)
        return pv[:, 0:LANES] * (1.0 / denom)

    def conv_ln(r0):
        acc = jnp.broadcast_to(cb_ref[...], (nslab, 1, CONV_ROWS, LANES))
        for tap in range(A_CONV):
            acc = acc + _rows(abuf, r0 + (A_HALO - A_CONV + 1) + tap, CONV_ROWS) * cw_ref[tap]
        y = _from_slabs(acc)
        mu = jnp.mean(y, axis=-1, keepdims=True)
        yc = y - mu
        y = yc * lax.rsqrt(jnp.mean(yc * yc, axis=-1, keepdims=True) + LN_EPS)
        y = y * lng_ref[...] + lnb_ref[...]
        mix_ref[r0:r0 + CONV_ROWS, 0:A_CH] = (y * jax.nn.sigmoid(y)).astype(jnp.bfloat16)

    rows = WINDOW

    def glu_in(j):
        r0 = j * rows
        _prenorm_rows(x_ref, gpre_ref, h_ref, r0, rows)
        za = jnp.dot(h_ref[r0:r0 + rows, :], win_ref[:, 0:2 * A_CH],
                     preferred_element_type=jnp.float32)
        _to_slabs(abuf, A_HALO + r0, za[:, 0:A_CH] * jax.nn.sigmoid(za[:, A_CH:2 * A_CH]))

    def qkv_in(j):
        r0 = j * rows
        t = None
        for piece in _split3_bf16(cs_ref[:, r0:r0 + rows]):
            d = lax.dot_general(piece, e_ref[...], (((0,), (0,)), ((), ())),
                                preferred_element_type=jnp.float32)
            t = d if t is None else t + d
        lane = lax.broadcasted_iota(jnp.int32, (rows, LANES), 1)
        c = t[:, 0:LANES] + jnp.where((lane & (HEAD_DIM - 1)) >= ROPE_DIM, 1.0, 0.0)
        s_up = t[:, LANES:2 * LANES]
        s_dn = t[:, 2 * LANES:3 * LANES]

        zq = jnp.dot(h_ref[r0:r0 + rows, :], win_ref[:, 2 * A_CH:EVEN_IN],
                     preferred_element_type=jnp.float32)
        for p in range(GROUP):
            q_ref[r0:r0 + rows, p * LANES:(p + 1) * LANES] = _rope(
                zq[:, p * LANES:(p + 1) * LANES], c, s_up, s_dn).astype(jnp.bfloat16)
        k = _rope(zq[:, Q_DIM:Q_DIM + KV_DIM], c, s_up, s_dn).astype(jnp.bfloat16)
        v = zq[:, Q_DIM + KV_DIM:Q_DIM + 2 * KV_DIM].astype(jnp.bfloat16)
        zero = jnp.zeros_like(k)
        for g in range(N_KV_HEADS):
            mine = (lane >= g * HEAD_DIM) & (lane < (g + 1) * HEAD_DIM)
            kg_ref[g, WINDOW + r0:WINDOW + r0 + rows, :] = jnp.where(mine, k, zero)
            vg_ref[g, WINDOW + r0:WINDOW + r0 + rows, 0:LANES] = jnp.where(mine, v, zero)

    def out(j):
        r0 = j * rows
        f_ref[r0:r0 + rows, :] = jnp.dot(mix_ref[r0:r0 + rows, :], wout_ref[...],
                                         preferred_element_type=jnp.float32)
        _residual_rows(x_ref, f_ref, gpost_ref, o_ref, r0, rows)

    nblk = tm // rows
    conv_per_blk = rows // CONV_ROWS
    for j in range(min(2, nblk)):
        glu_in(j)
        qkv_in(j)
    for j in range(nblk):
        o = attend(j, 0)
        if j + 2 < nblk:
            glu_in(j + 2)
        for cc in range(conv_per_blk // 2):
            conv_ln(j * rows + cc * CONV_ROWS)
        o = o + attend(j, 1)
        if j + 2 < nblk:
            qkv_in(j + 2)
        for cc in range(conv_per_blk // 2, conv_per_blk):
            conv_ln(j * rows + cc * CONV_ROWS)
        mix_ref[j * rows:(j + 1) * rows, A_CH:2 * A_CH] = jnp.concatenate(
            [o[p * WINDOW:(p + 1) * WINDOW] for p in range(GROUP)], axis=-1).astype(jnp.bfloat16)
        out(j)


def _pair_perm():
    idx = []
    for p in range(GROUP):
        for half in range(N_KV_HEADS):
            head = p + GROUP * half
            idx.extend(range(head * HEAD_DIM, (head + 1) * HEAD_DIM))
    return np.array(idx, dtype=np.int32)


def _even_mixer(x, cs, g_pre, g_post, w_in, conv_w, conv_b, ln_g, ln_b, sinks, w_out):
    b, s, d = x.shape
    tm = EVEN_TILE
    nslab = A_CH // LANES
    perm = _pair_perm()
    wq = w_in[:, 2 * A_CH:2 * A_CH + Q_DIM][:, perm] * (HEAD_DIM ** -0.5)
    w_in_p = jnp.concatenate([w_in[:, :2 * A_CH], wq, w_in[:, 2 * A_CH + Q_DIM:]],
                             axis=1).astype(jnp.bfloat16)
    w_out_p = jnp.concatenate([w_out[:A_CH], w_out[A_CH:][perm]], axis=0).astype(jnp.bfloat16)
    cw = conv_w.reshape(A_CONV, nslab, 1, 1, LANES)
    cb = conv_b.reshape(nslab, 1, 1, LANES)
    sink_rows = jnp.broadcast_to(
        jnp.repeat(sinks.reshape(N_KV_HEADS, GROUP), WINDOW, axis=1)[..., None],
        (N_KV_HEADS, GROUP * WINDOW, LANES))
    e = jnp.asarray(_rope_expand_matrix(), jnp.bfloat16)
    bias = jnp.asarray(_band_bias())
    steps = s // tm
    cs_spec = pl.BlockSpec((ROPE_DIM, tm), lambda bb, i: (0, bb * steps + i))
    resident = (4 * tm * d * 4 + 2 * ROPE_DIM * tm * 4 + w_in_p.size * 2 + w_out_p.size * 2
                + bias.size * 4 + nslab * (tm + A_HALO) * LANES * 4 + tm * d * 6 + tm * Q_DIM * 2
                + N_KV_HEADS * (tm + WINDOW) * 3 * LANES * 2 + tm * d * 2)
    return pl.pallas_call(
        functools.partial(_even_kernel, tm=tm),
        grid=(b, steps),
        in_specs=[
            _x_spec(tm, d), cs_spec,
            _const_spec(e.shape),
            _const_spec(bias.shape),
            _const_spec((1, d)),
            _const_spec((1, d)),
            _const_spec(w_in_p.shape),
            _const_spec(cw.shape),
            _const_spec(cb.shape),
            _const_spec((1, A_CH)),
            _const_spec((1, A_CH)),
            _const_spec(sink_rows.shape),
            _const_spec(w_out_p.shape),
        ],
        out_specs=_x_spec(tm, d),
        out_shape=jax.ShapeDtypeStruct(x.shape, x.dtype),
        scratch_shapes=[
            pltpu.VMEM((nslab, 1, tm + A_HALO, LANES), jnp.float32),
            pltpu.VMEM((tm, d), jnp.bfloat16),
            pltpu.VMEM((tm, Q_DIM), jnp.bfloat16),
            pltpu.VMEM((N_KV_HEADS, tm + WINDOW, LANES), jnp.bfloat16),
            pltpu.VMEM((N_KV_HEADS, tm + WINDOW, 2 * LANES), jnp.bfloat16),
            pltpu.VMEM((tm, 2 * A_CH), jnp.bfloat16),
            pltpu.VMEM((tm, d), jnp.float32),
        ],
        compiler_params=pltpu.CompilerParams(
            dimension_semantics=("arbitrary", "arbitrary"),
            vmem_limit_bytes=_vmem_limit(resident)),
        name="even_mixer",
    )(x, cs, e, bias, g_pre.reshape(1, d), g_post.reshape(1, d), w_in_p, cw, cb,
      ln_g.reshape(1, A_CH), ln_b.reshape(1, A_CH), sink_rows, w_out_p)


def kernel(x, positions, mix_norm_pre, mix_norm_post, ffn_norm_pre, ffn_norm_post, ev_w_in, ev_a_conv_w, ev_a_conv_b, ev_a_ln_g, ev_a_ln_b, ev_sinks, ev_w_out, od_w_in, od_conv_w, od_w_out, ffn_w_up, ffn_conv_w, ffn_w_down):
    depth = mix_norm_pre.shape[0]
    d = x.shape[2]
    assert x.shape[1] % SEQ_TILE == 0 and x.shape[1] % EVEN_TILE == 0 and EVEN_TILE % WINDOW == 0
    cs = _rope_tables(positions)
    bf = jnp.bfloat16
    nch = D_FF // FF_CHUNK
    w_up_c = ffn_w_up.reshape(depth, d, 2 * nch, FF_CHUNK).transpose(0, 2, 1, 3).astype(bf)
    w_down = ffn_w_down.astype(bf)
    for i in range(depth):
        j = i // 2
        if i % 2 == 0:
            x = _even_mixer(x, cs, mix_norm_pre[i], mix_norm_post[i], ev_w_in[j],
                            ev_a_conv_w[j], ev_a_conv_b[j], ev_a_ln_g[j], ev_a_ln_b[j],
                            ev_sinks[j], ev_w_out[j])
        else:
            x = _odd_mixer(x, mix_norm_pre[i], mix_norm_post[i], od_w_in[j].astype(bf),
                           od_conv_w[j], od_w_out[j].astype(bf))
        x = _ffn(x, i, ffn_norm_pre[i], ffn_norm_post[i], w_up_c, ffn_conv_w[i], w_down)
    return x
```

```python
import functools

import jax
import jax.numpy as jnp
import numpy as np
from jax import lax
from jax.experimental import pallas as pl
from jax.experimental.pallas import tpu as pltpu

D_MODEL = 1024
HEAD_DIM = 64
A_CH = 512
A_CONV = 31
N_Q_HEADS = 8
N_KV_HEADS = 2
GROUP = 4
WINDOW = 128
ROPE_THETA = 500000.0
ROPE_DIM = 16
ROPE_HALF = ROPE_DIM // 2
Q_DIM = 512
KV_DIM = 128
EVEN_IN = 1792
SC_DIM = 1024
D_FF = 2816
RMS_EPS = 1e-6
LN_EPS = 1e-5

LANES = 128
SUBLANES = 8
VMEM_BYTES = 64 << 20

SEQ_TILE = 512
EVEN_TILE = 1024
FF_CHUNK = 256
CONV_ROWS = 64
ROW_BLK = 64
EDGE_BLK = 128
FFN_SKEW = 3
DOWN_GROUP = 2
A_HALO = 32


def _rows(ref, start, size):
    return ref[:, pl.ds(0, 1, stride=2), pl.ds(start, size), :]


def _rms(x, g):
    ms = jnp.mean(x * x, axis=-1, keepdims=True)
    return x * lax.rsqrt(ms + RMS_EPS) * g


def _to_slabs(ref, row0, val):
    rows = val.shape[0]
    for k in range(val.shape[1] // LANES):
        ref[k, 0, row0:row0 + rows, :] = val[:, k * LANES:(k + 1) * LANES]


def _from_slabs(val):
    return jnp.concatenate([val[k, 0] for k in range(val.shape[0])], axis=-1)


def _conv3(ubuf, r0, rows, w):
    u0 = ubuf[:, :, SUBLANES + r0:SUBLANES + r0 + rows, :]
    u1 = _rows(ubuf, SUBLANES - 1 + r0, rows)
    u2 = _rows(ubuf, SUBLANES - 2 + r0, rows)
    return w[2] * u0 + w[1] * u1 + w[0] * u2


def _prenorm_rows(x_ref, g_ref, h_ref, r0, rows):
    g = g_ref[...]
    for r in range(r0, r0 + rows, ROW_BLK):
        h_ref[r:r + ROW_BLK, :] = _rms(x_ref[r:r + ROW_BLK, :], g).astype(jnp.bfloat16)


def _residual_rows(x_ref, f_ref, g_ref, o_ref, r0, rows):
    g = g_ref[...]
    for r in range(r0, r0 + rows, ROW_BLK):
        o_ref[r:r + ROW_BLK, :] = x_ref[r:r + ROW_BLK, :] + _rms(f_ref[r:r + ROW_BLK, :], g)


def _const_spec(shape):
    nd = len(shape)
    return pl.BlockSpec(shape, lambda b, i: (0,) * nd, pipeline_mode=pl.Buffered(1))


def _layer_spec(shape, layer):
    nd = len(shape)
    return pl.BlockSpec((None,) + tuple(shape[1:]), lambda b, i: (layer,) + (0,) * (nd - 1),
                        pipeline_mode=pl.Buffered(1))


def _x_spec(tm, width):
    return pl.BlockSpec((None, tm, width), lambda b, i: (b, i, 0))


def _vmem_limit(resident_bytes):
    return int(min(VMEM_BYTES - (4 << 20), 2 * resident_bytes + (8 << 20)))


def _rope_kernel(pos_ref, invf_ref, cs_ref):
    ang = invf_ref[...] * pos_ref[...]
    cs_ref[0:ROPE_HALF, :] = jnp.cos(ang)
    cs_ref[ROPE_HALF:ROPE_DIM, :] = jnp.sin(ang)


def _rope_tables(positions):
    inv_freq = ROPE_THETA ** (-(jnp.arange(ROPE_HALF, dtype=jnp.float32) * 2.0 / ROPE_DIM))
    pos = positions.astype(jnp.float32).reshape(1, -1)
    return pl.pallas_call(
        _rope_kernel,
        out_shape=jax.ShapeDtypeStruct((ROPE_DIM, pos.shape[1]), jnp.float32),
        name="rope_tables",
    )(pos, inv_freq.reshape(ROPE_HALF, 1))


def _rope_expand_matrix():
    e = np.zeros((ROPE_DIM, 3 * LANES), np.float32)
    for lane in range(LANES):
        d = lane % HEAD_DIM
        if d < ROPE_HALF:
            e[d, lane] = 1.0
            e[ROPE_HALF + d, LANES + lane] = -1.0
        elif d < ROPE_DIM:
            e[d - ROPE_HALF, lane] = 1.0
            e[d, 2 * LANES + lane] = 1.0
    return e


def _ffn_kernel(x_ref, gpre_ref, gpost_ref, wup_ref, cw_ref, wdown_ref, o_ref,
                ubuf, carry, h_ref, act_ref, f_ref, *, tm):
    i = pl.program_id(1)
    nk = FF_CHUNK // LANES
    nslab_half = D_FF // LANES
    nch = D_FF // FF_CHUNK

    @pl.when(i == 0)
    def _():
        carry[...] = jnp.zeros_like(carry)

    def up(c, ub, r0=0, rows=tm):
        sg = c * nk
        sv = nslab_half + c * nk
        if r0 == 0:
            ub[0:nk, :, 0:SUBLANES, :] = carry[sg:sg + nk]
            ub[nk:2 * nk, :, 0:SUBLANES, :] = carry[sv:sv + nk]
        h = h_ref[r0:r0 + rows, :]
        ug = jnp.dot(h, wup_ref[:, c * FF_CHUNK:(c + 1) * FF_CHUNK],
                     preferred_element_type=jnp.float32)
        uv = jnp.dot(h, wup_ref[:, D_FF + c * FF_CHUNK:D_FF + (c + 1) * FF_CHUNK],
                     preferred_element_type=jnp.float32)
        _to_slabs(ub.at[0:nk], SUBLANES + r0, ug)
        _to_slabs(ub.at[nk:2 * nk], SUBLANES + r0, uv)
        if r0 + rows == tm:
            carry[sg:sg + nk] = ub[0:nk, :, tm:tm + SUBLANES, :]
            carry[sv:sv + nk] = ub[nk:2 * nk, :, tm:tm + SUBLANES, :]

    def act(c, ub, r0=0, rows=tm):
        w = jnp.concatenate([cw_ref[:, c * nk:(c + 1) * nk],
                             cw_ref[:, nslab_half + c * nk:nslab_half + (c + 1) * nk]], axis=1)
        for r in range(r0, r0 + rows, ROW_BLK):
            y = _conv3(ub, r, ROW_BLK, w)
            yg = y[0:nk]
            a = (yg * jax.nn.sigmoid(yg) * y[nk:2 * nk]).astype(jnp.bfloat16)
            act_ref[r:r + ROW_BLK, c * FF_CHUNK:(c + 1) * FF_CHUNK] = _from_slabs(a)

    def down(c0, c1, r0=0, rows=tm):
        part = jnp.dot(act_ref[r0:r0 + rows, c0 * FF_CHUNK:c1 * FF_CHUNK],
                       wdown_ref[c0 * FF_CHUNK:c1 * FF_CHUNK, :],
                       preferred_element_type=jnp.float32)
        if c0 == 0:
            f_ref[r0:r0 + rows, :] = part
        else:
            f_ref[r0:r0 + rows, :] += part

    nbuf = FFN_SKEW + 1
    bufs = [ubuf.at[k] for k in range(nbuf)]
    for r0 in range(0, tm, EDGE_BLK):
        _prenorm_rows(x_ref, gpre_ref, h_ref, r0, EDGE_BLK)
        up(0, bufs[0], r0, EDGE_BLK)
    for c in range(1, FFN_SKEW):
        up(c, bufs[c])
    last = nch - 1
    for c in range(last):
        if c + FFN_SKEW < nch:
            up(c + FFN_SKEW, bufs[(c + FFN_SKEW) % nbuf])
        act(c, bufs[c % nbuf])
        if c % DOWN_GROUP == DOWN_GROUP - 1:
            down(c + 1 - DOWN_GROUP, c + 1)
    assert last % DOWN_GROUP == 0
    for r0 in range(0, tm, EDGE_BLK):
        act(last, bufs[last % nbuf], r0, EDGE_BLK)
        down(last, nch, r0, EDGE_BLK)
        _residual_rows(x_ref, f_ref, gpost_ref, o_ref, r0, EDGE_BLK)


def _ffn(x, layer, g_pre, g_post, w_up, conv_w, w_down):
    b, s, d = x.shape
    tm = SEQ_TILE
    nslab = 2 * D_FF // LANES
    nk = FF_CHUNK // LANES
    cw = conv_w.reshape(3, nslab, 1, 1, LANES)
    resident = (4 * tm * d * 4 + 2 * d * D_FF * 2 + D_FF * d * 2 + tm * D_FF * 2
                + tm * d * 6 + 4 * nk * (tm + SUBLANES) * LANES * 4)
    slab_buf = pltpu.VMEM((2 * nk, 1, tm + SUBLANES, LANES), jnp.float32)
    return pl.pallas_call(
        functools.partial(_ffn_kernel, tm=tm),
        grid=(b, s // tm),
        in_specs=[
            _x_spec(tm, d),
            _const_spec((1, d)),
            _const_spec((1, d)),
            _layer_spec(w_up.shape, layer),
            _const_spec(cw.shape),
            _layer_spec(w_down.shape, layer),
        ],
        out_specs=_x_spec(tm, d),
        out_shape=jax.ShapeDtypeStruct(x.shape, x.dtype),
        scratch_shapes=[
            pltpu.VMEM((FFN_SKEW + 1, 2 * nk, 1, tm + SUBLANES, LANES), jnp.float32),
            pltpu.VMEM((nslab, 1, SUBLANES, LANES), jnp.float32),
            pltpu.VMEM((tm, d), jnp.bfloat16),
            pltpu.VMEM((tm, D_FF), jnp.bfloat16),
            pltpu.VMEM((tm, d), jnp.float32),
        ],
        compiler_params=pltpu.CompilerParams(
            dimension_semantics=("arbitrary", "arbitrary"),
            vmem_limit_bytes=_vmem_limit(resident)),
        name="ffn",
    )(x, g_pre.reshape(1, d), g_post.reshape(1, d), w_up, cw, w_down)


def _odd_kernel(x_ref, gpre_ref, gpost_ref, win_ref, cw_ref, wout_ref, o_ref,
                ubuf, carry, h_ref, y_ref, f_ref, *, tm):
    i = pl.program_id(1)
    nk = FF_CHUNK // LANES

    @pl.when(i == 0)
    def _():
        carry[...] = jnp.zeros_like(carry)

    nch = SC_DIM // FF_CHUNK

    def up(c, ub, r0=0, rows=tm):
        col = c * FF_CHUNK
        s0 = c * nk
        if r0 == 0:
            ub[0:nk, :, 0:SUBLANES, :] = carry[s0:s0 + nk]
        h = h_ref[r0:r0 + rows, :]
        zb = jnp.dot(h, win_ref[:, col:col + FF_CHUNK], preferred_element_type=jnp.float32)
        zc = jnp.dot(h, win_ref[:, SC_DIM + col:SC_DIM + col + FF_CHUNK],
                     preferred_element_type=jnp.float32)
        zu = jnp.dot(h, win_ref[:, 2 * SC_DIM + col:2 * SC_DIM + col + FF_CHUNK],
                     preferred_element_type=jnp.float32)
        _to_slabs(ub.at[0:nk], SUBLANES + r0, zc * zu)
        _to_slabs(ub.at[nk:2 * nk], SUBLANES + r0, zb)
        if r0 + rows == tm:
            carry[s0:s0 + nk] = ub[0:nk, :, tm:tm + SUBLANES, :]

    def act(c, ub, r0=0, rows=tm):
        w = cw_ref[:, c * nk:(c + 1) * nk]
        for r in range(r0, r0 + rows, ROW_BLK):
            y = _conv3(ub.at[0:nk], r, ROW_BLK, w)
            b_gate = ub[nk:2 * nk, :, SUBLANES + r:SUBLANES + r + ROW_BLK, :]
            y_ref[r:r + ROW_BLK, c * FF_CHUNK:(c + 1) * FF_CHUNK] = _from_slabs(
                (b_gate * y).astype(jnp.bfloat16))

    _prenorm_rows(x_ref, gpre_ref, h_ref, 0, tm)
    for c in range(nch):
        ub = ubuf.at[c % 2]
        up(c, ub)
        act(c, ub)
    f_ref[...] = jnp.dot(y_ref[...], wout_ref[...], preferred_element_type=jnp.float32)
    _residual_rows(x_ref, f_ref, gpost_ref, o_ref, 0, tm)


def _odd_mixer(x, g_pre, g_post, w_in, conv_w, w_out):
    b, s, d = x.shape
    tm = SEQ_TILE
    nslab = SC_DIM // LANES
    nk = FF_CHUNK // LANES
    cw = conv_w.reshape(3, nslab, 1, 1, LANES)
    resident = (4 * tm * d * 4 + w_in.size * 2 + w_out.size * 2 + tm * SC_DIM * 2
                + tm * d * 6 + 4 * nk * (tm + SUBLANES) * LANES * 4)
    return pl.pallas_call(
        functools.partial(_odd_kernel, tm=tm),
        grid=(b, s // tm),
        in_specs=[
            _x_spec(tm, d),
            _const_spec((1, d)),
            _const_spec((1, d)),
            _const_spec(w_in.shape),
            _const_spec(cw.shape),
            _const_spec(w_out.shape),
        ],
        out_specs=_x_spec(tm, d),
        out_shape=jax.ShapeDtypeStruct(x.shape, x.dtype),
        scratch_shapes=[
            pltpu.VMEM((2, 2 * nk, 1, tm + SUBLANES, LANES), jnp.float32),
            pltpu.VMEM((nslab, 1, SUBLANES, LANES), jnp.float32),
            pltpu.VMEM((tm, d), jnp.bfloat16),
            pltpu.VMEM((tm, SC_DIM), jnp.bfloat16),
            pltpu.VMEM((tm, d), jnp.float32),
        ],
        compiler_params=pltpu.CompilerParams(
            dimension_semantics=("arbitrary", "arbitrary"),
            vmem_limit_bytes=_vmem_limit(resident)),
        name="odd_mixer",
    )(x, g_pre.reshape(1, d), g_post.reshape(1, d), w_in, cw, w_out)


def _split3_bf16(x):
    hi = x.astype(jnp.bfloat16)
    r = x - hi.astype(jnp.float32)
    mid = r.astype(jnp.bfloat16)
    lo = (r - mid.astype(jnp.float32)).astype(jnp.bfloat16)
    return hi, mid, lo


def _rope(x, c, s_up, s_dn):
    up = pltpu.roll(x, LANES - ROPE_HALF, axis=1)
    dn = pltpu.roll(x, ROPE_HALF, axis=1)
    return x * c + up * s_up + dn * s_dn


def _band_bias():
    r = np.arange(GROUP * WINDOW)[:, None] % WINDOW
    col = np.arange(2 * WINDOW)[None, :]
    band = (col > r) & (col <= r + WINDOW)
    first = band & (col >= WINDOW)
    return np.where(np.stack([band, first]), 0.0, -np.inf).astype(np.float32)


def _even_kernel(x_ref, cs_ref, e_ref, bias_ref, gpre_ref, gpost_ref, win_ref, cw_ref, cb_ref,
                 lng_ref, lnb_ref, sink_ref, wout_ref, o_ref,
                 abuf, h_ref, q_ref, kg_ref, vg_ref, mix_ref, f_ref, *, tm):
    i = pl.program_id(1)
    nslab = A_CH // LANES

    @pl.when(i == 0)
    def _():
        abuf[:, :, 0:A_HALO, :] = jnp.zeros((nslab, 1, A_HALO, LANES), jnp.float32)
        kg_ref[:, 0:WINDOW, :] = jnp.zeros((N_KV_HEADS, WINDOW, LANES), jnp.bfloat16)
        vg_ref[:, 0:WINDOW, 0:LANES] = jnp.zeros((N_KV_HEADS, WINDOW, LANES), jnp.bfloat16)
        vg_ref[:, :, LANES:2 * LANES] = jnp.ones((N_KV_HEADS, tm + WINDOW, LANES), jnp.bfloat16)

    @pl.when(i > 0)
    def _():
        abuf[:, :, 0:A_HALO, :] = abuf[:, :, tm:tm + A_HALO, :]
        kg_ref[:, 0:WINDOW, :] = kg_ref[:, tm:tm + WINDOW, :]
        vg_ref[:, 0:WINDOW, 0:LANES] = vg_ref[:, tm:tm + WINDOW, 0:LANES]

    def attend(j, g):
        qs = jnp.concatenate(
            [q_ref[j * WINDOW:(j + 1) * WINDOW, p * LANES:(p + 1) * LANES] for p in range(GROUP)],
            axis=0)
        kw = kg_ref[g, j * WINDOW:(j + 2) * WINDOW, :]
        vw = vg_ref[g, j * WINDOW:(j + 2) * WINDOW, :]
        sc = lax.dot_general(qs, kw, (((1,), (1,)), ((), ())),
                             preferred_element_type=jnp.float32)
        sc = sc + (bias_ref[jnp.where(i > 0, 0, 1)] if j == 0 else bias_ref[0])
        sink = sink_ref[g]
        m = jnp.maximum(jnp.max(sc, axis=-1, keepdims=True), sink)
        p_un = jnp.exp(sc - jnp.concatenate([m, m], axis=-1))
        pv = jnp.dot(p_un.astype(jnp.bfloat16), vw, preferred_element_type=jnp.float32)
        denom = pv[:, LANES:2 * LANES] + jnp.exp(sink - m)
        return pv[:, 0:LANES] * (1.0 / denom)

    def conv_ln(r0):
        acc = jnp.broadcast_to(cb_ref[...], (nslab, 1, CONV_ROWS, LANES))
        for tap in range(A_CONV):
            acc = acc + _rows(abuf, r0 + (A_HALO - A_CONV + 1) + tap, CONV_ROWS) * cw_ref[tap]
        y = _from_slabs(acc)
        mu = jnp.mean(y, axis=-1, keepdims=True)
        yc = y - mu
        y = yc * lax.rsqrt(jnp.mean(yc * yc, axis=-1, keepdims=True) + LN_EPS)
        y = y * lng_ref[...] + lnb_ref[...]
        mix_ref[r0:r0 + CONV_ROWS, 0:A_CH] = (y * jax.nn.sigmoid(y)).astype(jnp.bfloat16)

    rows = WINDOW

    def glu_in(j):
        r0 = j * rows
        _prenorm_rows(x_ref, gpre_ref, h_ref, r0, rows)
        za = jnp.dot(h_ref[r0:r0 + rows, :], win_ref[:, 0:2 * A_CH],
                     preferred_element_type=jnp.float32)
        _to_slabs(abuf, A_HALO + r0, za[:, 0:A_CH] * jax.nn.sigmoid(za[:, A_CH:2 * A_CH]))

    def qkv_in(j):
        r0 = j * rows
        t = None
        for piece in _split3_bf16(cs_ref[:, r0:r0 + rows]):
            d = lax.dot_general(piece, e_ref[...], (((0,), (0,)), ((), ())),
                                preferred_element_type=jnp.float32)
            t = d if t is None else t + d
        lane = lax.broadcasted_iota(jnp.int32, (rows, LANES), 1)
        c = t[:, 0:LANES] + jnp.where((lane & (HEAD_DIM - 1)) >= ROPE_DIM, 1.0, 0.0)
        s_up = t[:, LANES:2 * LANES]
        s_dn = t[:, 2 * LANES:3 * LANES]

        zq = jnp.dot(h_ref[r0:r0 + rows, :], win_ref[:, 2 * A_CH:EVEN_IN],
                     preferred_element_type=jnp.float32)
        for p in range(GROUP):
            q_ref[r0:r0 + rows, p * LANES:(p + 1) * LANES] = _rope(
                zq[:, p * LANES:(p + 1) * LANES], c, s_up, s_dn).astype(jnp.bfloat16)
        k = _rope(zq[:, Q_DIM:Q_DIM + KV_DIM], c, s_up, s_dn).astype(jnp.bfloat16)
        v = zq[:, Q_DIM + KV_DIM:Q_DIM + 2 * KV_DIM].astype(jnp.bfloat16)
        zero = jnp.zeros_like(k)
        for g in range(N_KV_HEADS):
            mine = (lane >= g * HEAD_DIM) & (lane < (g + 1) * HEAD_DIM)
            kg_ref[g, WINDOW + r0:WINDOW + r0 + rows, :] = jnp.where(mine, k, zero)
            vg_ref[g, WINDOW + r0:WINDOW + r0 + rows, 0:LANES] = jnp.where(mine, v, zero)

    def out(j):
        r0 = j * rows
        f_ref[r0:r0 + rows, :] = jnp.dot(mix_ref[r0:r0 + rows, :], wout_ref[...],
                                         preferred_element_type=jnp.float32)
        _residual_rows(x_ref, f_ref, gpost_ref, o_ref, r0, rows)

    nblk = tm // rows
    conv_per_blk = rows // CONV_ROWS
    for j in range(min(2, nblk)):
        glu_in(j)
        qkv_in(j)
    for j in range(nblk):
        o = attend(j, 0)
        if j + 2 < nblk:
            glu_in(j + 2)
        for cc in range(conv_per_blk // 2):
            conv_ln(j * rows + cc * CONV_ROWS)
        o = o + attend(j, 1)
        if j + 2 < nblk:
            qkv_in(j + 2)
        for cc in range(conv_per_blk // 2, conv_per_blk):
            conv_ln(j * rows + cc * CONV_ROWS)
        mix_ref[j * rows:(j + 1) * rows, A_CH:2 * A_CH] = jnp.concatenate(
            [o[p * WINDOW:(p + 1) * WINDOW] for p in range(GROUP)], axis=-1).astype(jnp.bfloat16)
        out(j)


def _pair_perm():
    idx = []
    for p in range(GROUP):
        for half in range(N_KV_HEADS):
            head = p + GROUP * half
            idx.extend(range(head * HEAD_DIM, (head + 1) * HEAD_DIM))
    return np.array(idx, dtype=np.int32)


def _even_mixer(x, cs, g_pre, g_post, w_in, conv_w, conv_b, ln_g, ln_b, sinks, w_out):
    b, s, d = x.shape
    tm = EVEN_TILE
    nslab = A_CH // LANES
    perm = _pair_perm()
    wq = w_in[:, 2 * A_CH:2 * A_CH + Q_DIM][:, perm] * (HEAD_DIM ** -0.5)
    w_in_p = jnp.concatenate([w_in[:, :2 * A_CH], wq, w_in[:, 2 * A_CH + Q_DIM:]],
                             axis=1).astype(jnp.bfloat16)
    w_out_p = jnp.concatenate([w_out[:A_CH], w_out[A_CH:][perm]], axis=0).astype(jnp.bfloat16)
    cw = conv_w.reshape(A_CONV, nslab, 1, 1, LANES)
    cb = conv_b.reshape(nslab, 1, 1, LANES)
    sink_rows = jnp.broadcast_to(
        jnp.repeat(sinks.reshape(N_KV_HEADS, GROUP), WINDOW, axis=1)[..., None],
        (N_KV_HEADS, GROUP * WINDOW, LANES))
    e = jnp.asarray(_rope_expand_matrix(), jnp.bfloat16)
    bias = jnp.asarray(_band_bias())
    steps = s // tm
    cs_spec = pl.BlockSpec((ROPE_DIM, tm), lambda bb, i: (0, bb * steps + i))
    resident = (4 * tm * d * 4 + 2 * ROPE_DIM * tm * 4 + w_in_p.size * 2 + w_out_p.size * 2
                + bias.size * 4 + nslab * (tm + A_HALO) * LANES * 4 + tm * d * 6 + tm * Q_DIM * 2
                + N_KV_HEADS * (tm + WINDOW) * 3 * LANES * 2 + tm * d * 2)
    return pl.pallas_call(
        functools.partial(_even_kernel, tm=tm),
        grid=(b, steps),
        in_specs=[
            _x_spec(tm, d), cs_spec,
            _const_spec(e.shape),
            _const_spec(bias.shape),
            _const_spec((1, d)),
            _const_spec((1, d)),
            _const_spec(w_in_p.shape),
            _const_spec(cw.shape),
            _const_spec(cb.shape),
            _const_spec((1, A_CH)),
            _const_spec((1, A_CH)),
            _const_spec(sink_rows.shape),
            _const_spec(w_out_p.shape),
        ],
        out_specs=_x_spec(tm, d),
        out_shape=jax.ShapeDtypeStruct(x.shape, x.dtype),
        scratch_shapes=[
            pltpu.VMEM((nslab, 1, tm + A_HALO, LANES), jnp.float32),
            pltpu.VMEM((tm, d), jnp.bfloat16),
            pltpu.VMEM((tm, Q_DIM), jnp.bfloat16),
            pltpu.VMEM((N_KV_HEADS, tm + WINDOW, LANES), jnp.bfloat16),
            pltpu.VMEM((N_KV_HEADS, tm + WINDOW, 2 * LANES), jnp.bfloat16),
            pltpu.VMEM((tm, 2 * A_CH), jnp.bfloat16),
            pltpu.VMEM((tm, d), jnp.float32),
        ],
        compiler_params=pltpu.CompilerParams(
            dimension_semantics=("arbitrary", "arbitrary"),
            vmem_limit_bytes=_vmem_limit(resident)),
        name="even_mixer",
    )(x, cs, e, bias, g_pre.reshape(1, d), g_post.reshape(1, d), w_in_p, cw, cb,
      ln_g.reshape(1, A_CH), ln_b.reshape(1, A_CH), sink_rows, w_out_p)


def kernel(x, positions, mix_norm_pre, mix_norm_post, ffn_norm_pre, ffn_norm_post, ev_w_in, ev_a_conv_w, ev_a_conv_b, ev_a_ln_g, ev_a_ln_b, ev_sinks, ev_w_out, od_w_in, od_conv_w, od_w_out, ffn_w_up, ffn_conv_w, ffn_w_down):
    depth = mix_norm_pre.shape[0]
    d = x.shape[2]
    assert x.shape[1] % SEQ_TILE == 0 and x.shape[1] % EVEN_TILE == 0 and EVEN_TILE % WINDOW == 0
    cs = _rope_tables(positions)
    bf = jnp.bfloat16
    w_up = ffn_w_up.astype(bf)
    w_down = ffn_w_down.astype(bf)
    for i in range(depth):
        j = i // 2
        if i % 2 == 0:
            x = _even_mixer(x, cs, mix_norm_pre[i], mix_norm_post[i], ev_w_in[j],
                            ev_a_conv_w[j], ev_a_conv_b[j], ev_a_ln_g[j], ev_a_ln_b[j],
                            ev_sinks[j], ev_w_out[j])
        else:
            x = _odd_mixer(x, mix_norm_pre[i], mix_norm_post[i], od_w_in[j].astype(bf),
                           od_conv_w[j], od_w_out[j].astype(bf))
        x = _ffn(x, i, ffn_norm_pre[i], ffn_norm_post[i], w_up, ffn_conv_w[i], w_down)
    return x
```

```python
import functools

import jax
import jax.numpy as jnp
import numpy as np
from jax import lax
from jax.experimental import pallas as pl
from jax.experimental.pallas import tpu as pltpu

D_MODEL = 1024
HEAD_DIM = 64
A_CH = 512
A_CONV = 31
N_Q_HEADS = 8
N_KV_HEADS = 2
GROUP = 4
WINDOW = 128
ROPE_THETA = 500000.0
ROPE_DIM = 16
ROPE_HALF = ROPE_DIM // 2
Q_DIM = 512
KV_DIM = 128
EVEN_IN = 1792
SC_DIM = 1024
D_FF = 2816
RMS_EPS = 1e-6
LN_EPS = 1e-5

LANES = 128
SUBLANES = 8
VMEM_BYTES = 64 << 20

SEQ_TILE = 1024
FFN_TILE = 1024
EVEN_TILE = 1024
FF_CHUNK = 256
CONV_ROWS = 64
ROW_BLK = 64
EDGE_BLK = 128
FFN_SKEW = 2
DOWN_GROUP = 2
A_HALO = 32


def _rows(ref, start, size):
    return ref[:, pl.ds(0, 1, stride=2), pl.ds(start, size), :]


def _rms(x, g):
    ms = jnp.mean(x * x, axis=-1, keepdims=True)
    return x * lax.rsqrt(ms + RMS_EPS) * g


def _to_slabs(ref, row0, val):
    rows = val.shape[0]
    for k in range(val.shape[1] // LANES):
        ref[k, 0, row0:row0 + rows, :] = val[:, k * LANES:(k + 1) * LANES]


def _from_slabs(val):
    return jnp.concatenate([val[k, 0] for k in range(val.shape[0])], axis=-1)


def _conv3(ubuf, r0, rows, w):
    u0 = ubuf[:, :, SUBLANES + r0:SUBLANES + r0 + rows, :]
    u1 = _rows(ubuf, SUBLANES - 1 + r0, rows)
    u2 = _rows(ubuf, SUBLANES - 2 + r0, rows)
    return w[2] * u0 + w[1] * u1 + w[0] * u2


def _prenorm_rows(x_ref, g_ref, h_ref, r0, rows):
    g = g_ref[...]
    for r in range(r0, r0 + rows, ROW_BLK):
        h_ref[r:r + ROW_BLK, :] = _rms(x_ref[r:r + ROW_BLK, :], g).astype(jnp.bfloat16)


def _residual_rows(x_ref, f_ref, g_ref, o_ref, r0, rows):
    g = g_ref[...]
    for r in range(r0, r0 + rows, ROW_BLK):
        o_ref[r:r + ROW_BLK, :] = x_ref[r:r + ROW_BLK, :] + _rms(f_ref[r:r + ROW_BLK, :], g)


def _const_spec(shape):
    nd = len(shape)
    return pl.BlockSpec(shape, lambda b, i: (0,) * nd, pipeline_mode=pl.Buffered(1))


def _layer_spec(shape, layer):
    nd = len(shape)
    return pl.BlockSpec((None,) + tuple(shape[1:]), lambda b, i: (layer,) + (0,) * (nd - 1),
                        pipeline_mode=pl.Buffered(1))


def _x_spec(tm, width):
    return pl.BlockSpec((None, tm, width), lambda b, i: (b, i, 0))


def _vmem_limit(resident_bytes):
    return int(min(VMEM_BYTES - (4 << 20), 2 * resident_bytes + (8 << 20)))


def _rope_kernel(pos_ref, invf_ref, cs_ref):
    ang = invf_ref[...] * pos_ref[...]
    cs_ref[0:ROPE_HALF, :] = jnp.cos(ang)
    cs_ref[ROPE_HALF:ROPE_DIM, :] = jnp.sin(ang)


def _rope_tables(positions):
    inv_freq = ROPE_THETA ** (-(jnp.arange(ROPE_HALF, dtype=jnp.float32) * 2.0 / ROPE_DIM))
    pos = positions.astype(jnp.float32).reshape(1, -1)
    return pl.pallas_call(
        _rope_kernel,
        out_shape=jax.ShapeDtypeStruct((ROPE_DIM, pos.shape[1]), jnp.float32),
        name="rope_tables",
    )(pos, inv_freq.reshape(ROPE_HALF, 1))


def _rope_expand_matrix():
    e = np.zeros((ROPE_DIM, 3 * LANES), np.float32)
    for lane in range(LANES):
        d = lane % HEAD_DIM
        if d < ROPE_HALF:
            e[d, lane] = 1.0
            e[ROPE_HALF + d, LANES + lane] = -1.0
        elif d < ROPE_DIM:
            e[d - ROPE_HALF, lane] = 1.0
            e[d, 2 * LANES + lane] = 1.0
    return e


def _ffn_kernel(x_ref, gpre_ref, gpost_ref, wup_ref, cw_ref, wdown_ref, o_ref,
                ubuf, carry, h_ref, act_ref, f_ref, *, tm):
    i = pl.program_id(1)
    nk = FF_CHUNK // LANES
    nslab_half = D_FF // LANES
    nch = D_FF // FF_CHUNK

    @pl.when(i == 0)
    def _():
        carry[...] = jnp.zeros_like(carry)

    def up(c, ub, r0=0, rows=tm):
        sg = c * nk
        sv = nslab_half + c * nk
        if r0 == 0:
            ub[0:nk, :, 0:SUBLANES, :] = carry[sg:sg + nk]
            ub[nk:2 * nk, :, 0:SUBLANES, :] = carry[sv:sv + nk]
        h = h_ref[r0:r0 + rows, :]
        ug = jnp.dot(h, wup_ref[:, c * FF_CHUNK:(c + 1) * FF_CHUNK],
                     preferred_element_type=jnp.float32)
        uv = jnp.dot(h, wup_ref[:, D_FF + c * FF_CHUNK:D_FF + (c + 1) * FF_CHUNK],
                     preferred_element_type=jnp.float32)
        _to_slabs(ub.at[0:nk], SUBLANES + r0, ug)
        _to_slabs(ub.at[nk:2 * nk], SUBLANES + r0, uv)
        if r0 + rows == tm:
            carry[sg:sg + nk] = ub[0:nk, :, tm:tm + SUBLANES, :]
            carry[sv:sv + nk] = ub[nk:2 * nk, :, tm:tm + SUBLANES, :]

    def act(c, ub, r0=0, rows=tm):
        w = jnp.concatenate([cw_ref[:, c * nk:(c + 1) * nk],
                             cw_ref[:, nslab_half + c * nk:nslab_half + (c + 1) * nk]], axis=1)
        for r in range(r0, r0 + rows, ROW_BLK):
            y = _conv3(ub, r, ROW_BLK, w)
            yg = y[0:nk]
            a = (yg * jax.nn.sigmoid(yg) * y[nk:2 * nk]).astype(jnp.bfloat16)
            k = c % DOWN_GROUP
            act_ref[(c // DOWN_GROUP) % 2, r:r + ROW_BLK,
                    k * FF_CHUNK:(k + 1) * FF_CHUNK] = _from_slabs(a)

    def down(c0, c1, r0=0, rows=tm):
        part = jnp.dot(act_ref[(c0 // DOWN_GROUP) % 2, r0:r0 + rows, 0:(c1 - c0) * FF_CHUNK],
                       wdown_ref[c0 * FF_CHUNK:c1 * FF_CHUNK, :],
                       preferred_element_type=jnp.float32)
        if c0 == 0:
            f_ref[r0:r0 + rows, :] = part
        else:
            f_ref[r0:r0 + rows, :] += part

    nbuf = FFN_SKEW + 1
    bufs = [ubuf.at[k] for k in range(nbuf)]
    for r0 in range(0, tm, EDGE_BLK):
        _prenorm_rows(x_ref, gpre_ref, h_ref, r0, EDGE_BLK)
        up(0, bufs[0], r0, EDGE_BLK)
    for c in range(1, FFN_SKEW):
        up(c, bufs[c])
    last = nch - 1
    for c in range(last):
        if c + FFN_SKEW < nch:
            up(c + FFN_SKEW, bufs[(c + FFN_SKEW) % nbuf])
        act(c, bufs[c % nbuf])
        if c % DOWN_GROUP == DOWN_GROUP - 1:
            down(c + 1 - DOWN_GROUP, c + 1)
    assert last % DOWN_GROUP == 0
    for r0 in range(0, tm, EDGE_BLK):
        act(last, bufs[last % nbuf], r0, EDGE_BLK)
        down(last, nch, r0, EDGE_BLK)
        _residual_rows(x_ref, f_ref, gpost_ref, o_ref, r0, EDGE_BLK)


def _ffn(x, layer, g_pre, g_post, w_up, conv_w, w_down):
    b, s, d = x.shape
    tm = FFN_TILE
    nslab = 2 * D_FF // LANES
    nk = FF_CHUNK // LANES
    cw = conv_w.reshape(3, nslab, 1, 1, LANES)
    resident = (4 * tm * d * 4 + 2 * d * D_FF * 2 + D_FF * d * 2
                + 2 * tm * DOWN_GROUP * FF_CHUNK * 2 + tm * d * 6
                + (FFN_SKEW + 1) * 2 * nk * (tm + SUBLANES) * LANES * 4)
    return pl.pallas_call(
        functools.partial(_ffn_kernel, tm=tm),
        grid=(b, s // tm),
        in_specs=[
            _x_spec(tm, d),
            _const_spec((1, d)),
            _const_spec((1, d)),
            _layer_spec(w_up.shape, layer),
            _const_spec(cw.shape),
            _layer_spec(w_down.shape, layer),
        ],
        out_specs=_x_spec(tm, d),
        out_shape=jax.ShapeDtypeStruct(x.shape, x.dtype),
        scratch_shapes=[
            pltpu.VMEM((FFN_SKEW + 1, 2 * nk, 1, tm + SUBLANES, LANES), jnp.float32),
            pltpu.VMEM((nslab, 1, SUBLANES, LANES), jnp.float32),
            pltpu.VMEM((tm, d), jnp.bfloat16),
            pltpu.VMEM((2, tm, DOWN_GROUP * FF_CHUNK), jnp.bfloat16),
            pltpu.VMEM((tm, d), jnp.float32),
        ],
        compiler_params=pltpu.CompilerParams(
            dimension_semantics=("arbitrary", "arbitrary"),
            vmem_limit_bytes=_vmem_limit(resident)),
        name="ffn",
    )(x, g_pre.reshape(1, d), g_post.reshape(1, d), w_up, cw, w_down)


def _odd_kernel(x_ref, gpre_ref, gpost_ref, win_ref, cw_ref, wout_ref, o_ref,
                ubuf, carry, h_ref, y_ref, f_ref, *, tm):
    i = pl.program_id(1)
    nk = FF_CHUNK // LANES

    @pl.when(i == 0)
    def _():
        carry[...] = jnp.zeros_like(carry)

    nch = SC_DIM // FF_CHUNK

    def up(c, ub, r0=0, rows=tm):
        col = c * FF_CHUNK
        s0 = c * nk
        if r0 == 0:
            ub[0:nk, :, 0:SUBLANES, :] = carry[s0:s0 + nk]
        h = h_ref[r0:r0 + rows, :]
        zb = jnp.dot(h, win_ref[:, col:col + FF_CHUNK], preferred_element_type=jnp.float32)
        zc = jnp.dot(h, win_ref[:, SC_DIM + col:SC_DIM + col + FF_CHUNK],
                     preferred_element_type=jnp.float32)
        zu = jnp.dot(h, win_ref[:, 2 * SC_DIM + col:2 * SC_DIM + col + FF_CHUNK],
                     preferred_element_type=jnp.float32)
        _to_slabs(ub.at[0:nk], SUBLANES + r0, zc * zu)
        _to_slabs(ub.at[nk:2 * nk], SUBLANES + r0, zb)
        if r0 + rows == tm:
            carry[s0:s0 + nk] = ub[0:nk, :, tm:tm + SUBLANES, :]

    def act(c, ub, r0=0, rows=tm):
        w = cw_ref[:, c * nk:(c + 1) * nk]
        for r in range(r0, r0 + rows, ROW_BLK):
            y = _conv3(ub.at[0:nk], r, ROW_BLK, w)
            b_gate = ub[nk:2 * nk, :, SUBLANES + r:SUBLANES + r + ROW_BLK, :]
            y_ref[r:r + ROW_BLK, c * FF_CHUNK:(c + 1) * FF_CHUNK] = _from_slabs(
                (b_gate * y).astype(jnp.bfloat16))

    _prenorm_rows(x_ref, gpre_ref, h_ref, 0, tm)
    for c in range(nch):
        ub = ubuf.at[c % 2]
        up(c, ub)
        act(c, ub)
    f_ref[...] = jnp.dot(y_ref[...], wout_ref[...], preferred_element_type=jnp.float32)
    _residual_rows(x_ref, f_ref, gpost_ref, o_ref, 0, tm)


def _odd_mixer(x, g_pre, g_post, w_in, conv_w, w_out):
    b, s, d = x.shape
    tm = SEQ_TILE
    nslab = SC_DIM // LANES
    nk = FF_CHUNK // LANES
    cw = conv_w.reshape(3, nslab, 1, 1, LANES)
    resident = (4 * tm * d * 4 + w_in.size * 2 + w_out.size * 2 + tm * SC_DIM * 2
                + tm * d * 6 + 4 * nk * (tm + SUBLANES) * LANES * 4)
    return pl.pallas_call(
        functools.partial(_odd_kernel, tm=tm),
        grid=(b, s // tm),
        in_specs=[
            _x_spec(tm, d),
            _const_spec((1, d)),
            _const_spec((1, d)),
            _const_spec(w_in.shape),
            _const_spec(cw.shape),
            _const_spec(w_out.shape),
        ],
        out_specs=_x_spec(tm, d),
        out_shape=jax.ShapeDtypeStruct(x.shape, x.dtype),
        scratch_shapes=[
            pltpu.VMEM((2, 2 * nk, 1, tm + SUBLANES, LANES), jnp.float32),
            pltpu.VMEM((nslab, 1, SUBLANES, LANES), jnp.float32),
            pltpu.VMEM((tm, d), jnp.bfloat16),
            pltpu.VMEM((tm, SC_DIM), jnp.bfloat16),
            pltpu.VMEM((tm, d), jnp.float32),
        ],
        compiler_params=pltpu.CompilerParams(
            dimension_semantics=("arbitrary", "arbitrary"),
            vmem_limit_bytes=_vmem_limit(resident)),
        name="odd_mixer",
    )(x, g_pre.reshape(1, d), g_post.reshape(1, d), w_in, cw, w_out)


def _split3_bf16(x):
    hi = x.astype(jnp.bfloat16)
    r = x - hi.astype(jnp.float32)
    mid = r.astype(jnp.bfloat16)
    lo = (r - mid.astype(jnp.float32)).astype(jnp.bfloat16)
    return hi, mid, lo


def _rope(x, c, s_up, s_dn):
    up = pltpu.roll(x, LANES - ROPE_HALF, axis=1)
    dn = pltpu.roll(x, ROPE_HALF, axis=1)
    return x * c + up * s_up + dn * s_dn


def _band_bias():
    r = np.arange(GROUP * WINDOW)[:, None] % WINDOW
    col = np.arange(2 * WINDOW)[None, :]
    band = (col > r) & (col <= r + WINDOW)
    first = band & (col >= WINDOW)
    return np.where(np.stack([band, first]), 0.0, -np.inf).astype(np.float32)


def _even_kernel(x_ref, cs_ref, e_ref, bias_ref, gpre_ref, gpost_ref, win_ref, cw_ref, cb_ref,
                 lng_ref, lnb_ref, sink_ref, wout_ref, o_ref,
                 abuf, h_ref, q_ref, kg_ref, vg_ref, mix_ref, f_ref, *, tm):
    i = pl.program_id(1)
    nslab = A_CH // LANES

    @pl.when(i == 0)
    def _():
        abuf[:, :, 0:A_HALO, :] = jnp.zeros((nslab, 1, A_HALO, LANES), jnp.float32)
        kg_ref[:, 0:WINDOW, :] = jnp.zeros((N_KV_HEADS, WINDOW, LANES), jnp.bfloat16)
        vg_ref[:, 0:WINDOW, 0:LANES] = jnp.zeros((N_KV_HEADS, WINDOW, LANES), jnp.bfloat16)
        vg_ref[:, :, LANES:2 * LANES] = jnp.ones((N_KV_HEADS, tm + WINDOW, LANES), jnp.bfloat16)

    @pl.when(i > 0)
    def _():
        abuf[:, :, 0:A_HALO, :] = abuf[:, :, tm:tm + A_HALO, :]
        kg_ref[:, 0:WINDOW, :] = kg_ref[:, tm:tm + WINDOW, :]
        vg_ref[:, 0:WINDOW, 0:LANES] = vg_ref[:, tm:tm + WINDOW, 0:LANES]

    def attend(j, g):
        qs = jnp.concatenate(
            [q_ref[j * WINDOW:(j + 1) * WINDOW, p * LANES:(p + 1) * LANES] for p in range(GROUP)],
            axis=0)
        kw = kg_ref[g, j * WINDOW:(j + 2) * WINDOW, :]
        vw = vg_ref[g, j * WINDOW:(j + 2) * WINDOW, :]
        sc = lax.dot_general(qs, kw, (((1,), (1,)), ((), ())),
                             preferred_element_type=jnp.float32)
        sc = sc + (bias_ref[jnp.where(i > 0, 0, 1)] if j == 0 else bias_ref[0])
        sink = sink_ref[g]
        m = jnp.maximum(jnp.max(sc, axis=-1, keepdims=True), sink)
        p_un = jnp.exp(sc - jnp.concatenate([m, m], axis=-1))
        pv = jnp.dot(p_un.astype(jnp.bfloat16), vw, preferred_element_type=jnp.float32)
        denom = pv[:, LANES:2 * LANES] + jnp.exp(sink - m)
        return pv[:, 0:LANES] * (1.0 / denom)

    def conv_ln(r0):
        acc = jnp.broadcast_to(cb_ref[...], (nslab, 1, CONV_ROWS, LANES))
        for tap in range(A_CONV):
            acc = acc + _rows(abuf, r0 + (A_HALO - A_CONV + 1) + tap, CONV_ROWS) * cw_ref[tap]
        y = _from_slabs(acc)
        mu = jnp.mean(y, axis=-1, keepdims=True)
        yc = y - mu
        y = yc * lax.rsqrt(jnp.mean(yc * yc, axis=-1, keepdims=True) + LN_EPS)
        y = y * lng_ref[...] + lnb_ref[...]
        mix_ref[r0:r0 + CONV_ROWS, 0:A_CH] = (y * jax.nn.sigmoid(y)).astype(jnp.bfloat16)

    rows = WINDOW

    def glu_in(j):
        r0 = j * rows
        _prenorm_rows(x_ref, gpre_ref, h_ref, r0, rows)
        za = jnp.dot(h_ref[r0:r0 + rows, :], win_ref[:, 0:2 * A_CH],
                     preferred_element_type=jnp.float32)
        _to_slabs(abuf, A_HALO + r0, za[:, 0:A_CH] * jax.nn.sigmoid(za[:, A_CH:2 * A_CH]))

    def qkv_in(j):
        r0 = j * rows
        t = None
        for piece in _split3_bf16(cs_ref[:, r0:r0 + rows]):
            d = lax.dot_general(piece, e_ref[...], (((0,), (0,)), ((), ())),
                                preferred_element_type=jnp.float32)
            t = d if t is None else t + d
        lane = lax.broadcasted_iota(jnp.int32, (rows, LANES), 1)
        c = t[:, 0:LANES] + jnp.where((lane & (HEAD_DIM - 1)) >= ROPE_DIM, 1.0, 0.0)
        s_up = t[:, LANES:2 * LANES]
        s_dn = t[:, 2 * LANES:3 * LANES]

        zq = jnp.dot(h_ref[r0:r0 + rows, :], win_ref[:, 2 * A_CH:EVEN_IN],
                     preferred_element_type=jnp.float32)
        for p in range(GROUP):
            q_ref[r0:r0 + rows, p * LANES:(p + 1) * LANES] = _rope(
                zq[:, p * LANES:(p + 1) * LANES], c, s_up, s_dn).astype(jnp.bfloat16)
        k = _rope(zq[:, Q_DIM:Q_DIM + KV_DIM], c, s_up, s_dn).astype(jnp.bfloat16)
        v = zq[:, Q_DIM + KV_DIM:Q_DIM + 2 * KV_DIM].astype(jnp.bfloat16)
        zero = jnp.zeros_like(k)
        for g in range(N_KV_HEADS):
            mine = (lane >= g * HEAD_DIM) & (lane < (g + 1) * HEAD_DIM)
            kg_ref[g, WINDOW + r0:WINDOW + r0 + rows, :] = jnp.where(mine, k, zero)
            vg_ref[g, WINDOW + r0:WINDOW + r0 + rows, 0:LANES] = jnp.where(mine, v, zero)

    def out(j):
        r0 = j * rows
        f_ref[r0:r0 + rows, :] = jnp.dot(mix_ref[r0:r0 + rows, :], wout_ref[...],
                                         preferred_element_type=jnp.float32)
        _residual_rows(x_ref, f_ref, gpost_ref, o_ref, r0, rows)

    nblk = tm // rows
    conv_per_blk = rows // CONV_ROWS
    for j in range(min(2, nblk)):
        glu_in(j)
        qkv_in(j)
    for j in range(nblk):
        o = attend(j, 0)
        if j + 2 < nblk:
            glu_in(j + 2)
        for cc in range(conv_per_blk // 2):
            conv_ln(j * rows + cc * CONV_ROWS)
        o = o + attend(j, 1)
        if j + 2 < nblk:
            qkv_in(j + 2)
        for cc in range(conv_per_blk // 2, conv_per_blk):
            conv_ln(j * rows + cc * CONV_ROWS)
        mix_ref[j * rows:(j + 1) * rows, A_CH:2 * A_CH] = jnp.concatenate(
            [o[p * WINDOW:(p + 1) * WINDOW] for p in range(GROUP)], axis=-1).astype(jnp.bfloat16)
        out(j)


def _pair_perm():
    idx = []
    for p in range(GROUP):
        for half in range(N_KV_HEADS):
            head = p + GROUP * half
            idx.extend(range(head * HEAD_DIM, (head + 1) * HEAD_DIM))
    return np.array(idx, dtype=np.int32)


def _even_mixer(x, cs, g_pre, g_post, w_in, conv_w, conv_b, ln_g, ln_b, sinks, w_out):
    b, s, d = x.shape
    tm = EVEN_TILE
    nslab = A_CH // LANES
    perm = _pair_perm()
    wq = w_in[:, 2 * A_CH:2 * A_CH + Q_DIM][:, perm] * (HEAD_DIM ** -0.5)
    w_in_p = jnp.concatenate([w_in[:, :2 * A_CH], wq, w_in[:, 2 * A_CH + Q_DIM:]],
                             axis=1).astype(jnp.bfloat16)
    w_out_p = jnp.concatenate([w_out[:A_CH], w_out[A_CH:][perm]], axis=0).astype(jnp.bfloat16)
    cw = conv_w.reshape(A_CONV, nslab, 1, 1, LANES)
    cb = conv_b.reshape(nslab, 1, 1, LANES)
    sink_rows = jnp.broadcast_to(
        jnp.repeat(sinks.reshape(N_KV_HEADS, GROUP), WINDOW, axis=1)[..., None],
        (N_KV_HEADS, GROUP * WINDOW, LANES))
    e = jnp.asarray(_rope_expand_matrix(), jnp.bfloat16)
    bias = jnp.asarray(_band_bias())
    steps = s // tm
    cs_spec = pl.BlockSpec((ROPE_DIM, tm), lambda bb, i: (0, bb * steps + i))
    resident = (4 * tm * d * 4 + 2 * ROPE_DIM * tm * 4 + w_in_p.size * 2 + w_out_p.size * 2
                + bias.size * 4 + nslab * (tm + A_HALO) * LANES * 4 + tm * d * 6 + tm * Q_DIM * 2
                + N_KV_HEADS * (tm + WINDOW) * 3 * LANES * 2 + tm * d * 2)
    return pl.pallas_call(
        functools.partial(_even_kernel, tm=tm),
        grid=(b, steps),
        in_specs=[
            _x_spec(tm, d), cs_spec,
            _const_spec(e.shape),
            _const_spec(bias.shape),
            _const_spec((1, d)),
            _const_spec((1, d)),
            _const_spec(w_in_p.shape),
            _const_spec(cw.shape),
            _const_spec(cb.shape),
            _const_spec((1, A_CH)),
            _const_spec((1, A_CH)),
            _const_spec(sink_rows.shape),
            _const_spec(w_out_p.shape),
        ],
        out_specs=_x_spec(tm, d),
        out_shape=jax.ShapeDtypeStruct(x.shape, x.dtype),
        scratch_shapes=[
            pltpu.VMEM((nslab, 1, tm + A_HALO, LANES), jnp.float32),
            pltpu.VMEM((tm, d), jnp.bfloat16),
            pltpu.VMEM((tm, Q_DIM), jnp.bfloat16),
            pltpu.VMEM((N_KV_HEADS, tm + WINDOW, LANES), jnp.bfloat16),
            pltpu.VMEM((N_KV_HEADS, tm + WINDOW, 2 * LANES), jnp.bfloat16),
            pltpu.VMEM((tm, 2 * A_CH), jnp.bfloat16),
            pltpu.VMEM((tm, d), jnp.float32),
        ],
        compiler_params=pltpu.CompilerParams(
            dimension_semantics=("arbitrary", "arbitrary"),
            vmem_limit_bytes=_vmem_limit(resident)),
        name="even_mixer",
    )(x, cs, e, bias, g_pre.reshape(1, d), g_post.reshape(1, d), w_in_p, cw, cb,
      ln_g.reshape(1, A_CH), ln_b.reshape(1, A_CH), sink_rows, w_out_p)


def kernel(x, positions, mix_norm_pre, mix_norm_post, ffn_norm_pre, ffn_norm_post, ev_w_in, ev_a_conv_w, ev_a_conv_b, ev_a_ln_g, ev_a_ln_b, ev_sinks, ev_w_out, od_w_in, od_conv_w, od_w_out, ffn_w_up, ffn_conv_w, ffn_w_down):
    depth = mix_norm_pre.shape[0]
    d = x.shape[2]
    assert all(x.shape[1] % t == 0 for t in (SEQ_TILE, EVEN_TILE, FFN_TILE))
    assert EVEN_TILE % WINDOW == 0
    cs = _rope_tables(positions)
    bf = jnp.bfloat16
    w_up = ffn_w_up.astype(bf)
    w_down = ffn_w_down.astype(bf)
    for i in range(depth):
        j = i // 2
        if i % 2 == 0:
            x = _even_mixer(x, cs, mix_norm_pre[i], mix_norm_post[i], ev_w_in[j],
                            ev_a_conv_w[j], ev_a_conv_b[j], ev_a_ln_g[j], ev_a_ln_b[j],
                            ev_sinks[j], ev_w_out[j])
        else:
            x = _odd_mixer(x, mix_norm_pre[i], mix_norm_post[i], od_w_in[j].astype(bf),
                           od_conv_w[j], od_w_out[j].astype(bf))
        x = _ffn(x, i, ffn_norm_pre[i], ffn_norm_post[i], w_up, ffn_conv_w[i], w_down)
    return x
```

```python
import functools

import jax
import jax.numpy as jnp
import numpy as np
from jax import lax
from jax.experimental import pallas as pl
from jax.experimental.pallas import tpu as pltpu

D_MODEL = 1024
HEAD_DIM = 64
A_CH = 512
A_CONV = 31
N_Q_HEADS = 8
N_KV_HEADS = 2
GROUP = 4
WINDOW = 128
ROPE_THETA = 500000.0
ROPE_DIM = 16
ROPE_HALF = ROPE_DIM // 2
Q_DIM = 512
KV_DIM = 128
EVEN_IN = 1792
SC_DIM = 1024
D_FF = 2816
RMS_EPS = 1e-6
LN_EPS = 1e-5

LANES = 128
SUBLANES = 8
VMEM_BYTES = 64 << 20

SEQ_TILE = 1024
FFN_TILE = 512
EVEN_TILE = 1024
FF_CHUNK = 256
CONV_ROWS = 64
ROW_BLK = 64
EDGE_BLK = 128
FFN_SKEW = 3
DOWN_GROUP = 2
TAIL_GROUP = 1
TAIL_BLK = 128
A_HALO = 32


def _rows(ref, start, size):
    return ref[:, pl.ds(0, 1, stride=2), pl.ds(start, size), :]


def _rms(x, g):
    ms = jnp.mean(x * x, axis=-1, keepdims=True)
    return x * lax.rsqrt(ms + RMS_EPS) * g


def _to_slabs(ref, row0, val):
    rows = val.shape[0]
    for k in range(val.shape[1] // LANES):
        ref[k, 0, row0:row0 + rows, :] = val[:, k * LANES:(k + 1) * LANES]


def _from_slabs(val):
    return jnp.concatenate([val[k, 0] for k in range(val.shape[0])], axis=-1)


def _conv3(ubuf, r0, rows, w):
    u0 = ubuf[:, :, SUBLANES + r0:SUBLANES + r0 + rows, :]
    u1 = _rows(ubuf, SUBLANES - 1 + r0, rows)
    u2 = _rows(ubuf, SUBLANES - 2 + r0, rows)
    return w[2] * u0 + w[1] * u1 + w[0] * u2


def _prenorm_rows(x_ref, g_ref, h_ref, r0, rows):
    g = g_ref[...]
    for r in range(r0, r0 + rows, ROW_BLK):
        h_ref[r:r + ROW_BLK, :] = _rms(x_ref[r:r + ROW_BLK, :], g).astype(jnp.bfloat16)


def _residual_rows(x_ref, f_ref, g_ref, o_ref, r0, rows):
    g = g_ref[...]
    for r in range(r0, r0 + rows, ROW_BLK):
        o_ref[r:r + ROW_BLK, :] = x_ref[r:r + ROW_BLK, :] + _rms(f_ref[r:r + ROW_BLK, :], g)


def _const_spec(shape):
    nd = len(shape)
    return pl.BlockSpec(shape, lambda b, i: (0,) * nd, pipeline_mode=pl.Buffered(1))


def _layer_spec(shape, layer):
    nd = len(shape)
    return pl.BlockSpec((None,) + tuple(shape[1:]), lambda b, i: (layer,) + (0,) * (nd - 1),
                        pipeline_mode=pl.Buffered(1))


def _x_spec(tm, width):
    return pl.BlockSpec((None, tm, width), lambda b, i: (b, i, 0))


def _vmem_limit(resident_bytes):
    return int(min(VMEM_BYTES - (4 << 20), 2 * resident_bytes + (8 << 20)))


def _rope_kernel(pos_ref, invf_ref, cs_ref):
    ang = invf_ref[...] * pos_ref[...]
    cs_ref[0:ROPE_HALF, :] = jnp.cos(ang)
    cs_ref[ROPE_HALF:ROPE_DIM, :] = jnp.sin(ang)


def _rope_tables(positions):
    inv_freq = ROPE_THETA ** (-(jnp.arange(ROPE_HALF, dtype=jnp.float32) * 2.0 / ROPE_DIM))
    pos = positions.astype(jnp.float32).reshape(1, -1)
    return pl.pallas_call(
        _rope_kernel,
        out_shape=jax.ShapeDtypeStruct((ROPE_DIM, pos.shape[1]), jnp.float32),
        name="rope_tables",
    )(pos, inv_freq.reshape(ROPE_HALF, 1))


def _rope_expand_matrix():
    e = np.zeros((ROPE_DIM, 3 * LANES), np.float32)
    for lane in range(LANES):
        d = lane % HEAD_DIM
        if d < ROPE_HALF:
            e[d, lane] = 1.0
            e[ROPE_HALF + d, LANES + lane] = -1.0
        elif d < ROPE_DIM:
            e[d - ROPE_HALF, lane] = 1.0
            e[d, 2 * LANES + lane] = 1.0
    return e


def _down_groups(nch):
    cuts = list(range(0, nch - TAIL_GROUP + 1, DOWN_GROUP)) + [nch]
    groups = list(zip(cuts[:-1], cuts[1:]))
    assert groups[-1][1] - groups[-1][0] == TAIL_GROUP
    return groups


def _ffn_kernel(x_ref, gpre_ref, gpost_ref, wup_ref, cw_ref, wdown_ref, o_ref,
                ubuf, carry, h_ref, act_ref, f_ref, *, tm):
    i = pl.program_id(1)
    nk = FF_CHUNK // LANES
    nslab_half = D_FF // LANES
    nch = D_FF // FF_CHUNK
    groups = _down_groups(nch)
    group_of = [g for g, (c0, c1) in enumerate(groups) for _ in range(c0, c1)]

    @pl.when(i == 0)
    def _():
        carry[...] = jnp.zeros_like(carry)

    def up(c, ub, r0=0, rows=tm):
        sg = c * nk
        sv = nslab_half + c * nk
        if r0 == 0:
            ub[0:nk, :, 0:SUBLANES, :] = carry[sg:sg + nk]
            ub[nk:2 * nk, :, 0:SUBLANES, :] = carry[sv:sv + nk]
        h = h_ref[r0:r0 + rows, :]
        ug = jnp.dot(h, wup_ref[:, c * FF_CHUNK:(c + 1) * FF_CHUNK],
                     preferred_element_type=jnp.float32)
        uv = jnp.dot(h, wup_ref[:, D_FF + c * FF_CHUNK:D_FF + (c + 1) * FF_CHUNK],
                     preferred_element_type=jnp.float32)
        _to_slabs(ub.at[0:nk], SUBLANES + r0, ug)
        _to_slabs(ub.at[nk:2 * nk], SUBLANES + r0, uv)
        if r0 + rows == tm:
            carry[sg:sg + nk] = ub[0:nk, :, tm:tm + SUBLANES, :]
            carry[sv:sv + nk] = ub[nk:2 * nk, :, tm:tm + SUBLANES, :]

    def act(c, ub, r0=0, rows=tm):
        w = jnp.concatenate([cw_ref[:, c * nk:(c + 1) * nk],
                             cw_ref[:, nslab_half + c * nk:nslab_half + (c + 1) * nk]], axis=1)
        for r in range(r0, r0 + rows, ROW_BLK):
            y = _conv3(ub, r, ROW_BLK, w)
            yg = y[0:nk]
            a = (yg * jax.nn.sigmoid(yg) * y[nk:2 * nk]).astype(jnp.bfloat16)
            g = group_of[c]
            k = c - groups[g][0]
            act_ref[g % 2, r:r + ROW_BLK, k * FF_CHUNK:(k + 1) * FF_CHUNK] = _from_slabs(a)

    def down(g, r0=0, rows=tm):
        c0, c1 = groups[g]
        part = jnp.dot(act_ref[g % 2, r0:r0 + rows, 0:(c1 - c0) * FF_CHUNK],
                       wdown_ref[c0 * FF_CHUNK:c1 * FF_CHUNK, :],
                       preferred_element_type=jnp.float32)
        if c0 == 0:
            f_ref[r0:r0 + rows, :] = part
        else:
            f_ref[r0:r0 + rows, :] += part

    nbuf = FFN_SKEW + 1
    bufs = [ubuf.at[k] for k in range(nbuf)]
    for r0 in range(0, tm, EDGE_BLK):
        _prenorm_rows(x_ref, gpre_ref, h_ref, r0, EDGE_BLK)
        up(0, bufs[0], r0, EDGE_BLK)
    for c in range(1, FFN_SKEW):
        up(c, bufs[c])
    last = nch - 1
    for c in range(last):
        if c + FFN_SKEW < nch:
            up(c + FFN_SKEW, bufs[(c + FFN_SKEW) % nbuf])
        act(c, bufs[c % nbuf])
        g = group_of[c]
        if c + 1 == groups[g][1]:
            down(g)
    for r0 in range(0, tm, TAIL_BLK):
        act(last, bufs[last % nbuf], r0, TAIL_BLK)
        down(len(groups) - 1, r0, TAIL_BLK)
        _residual_rows(x_ref, f_ref, gpost_ref, o_ref, r0, TAIL_BLK)


def _ffn(x, layer, g_pre, g_post, w_up, conv_w, w_down):
    b, s, d = x.shape
    tm = FFN_TILE
    nslab = 2 * D_FF // LANES
    nk = FF_CHUNK // LANES
    cw = conv_w.reshape(3, nslab, 1, 1, LANES)
    resident = (4 * tm * d * 4 + 2 * d * D_FF * 2 + D_FF * d * 2
                + 2 * tm * max(DOWN_GROUP, TAIL_GROUP) * FF_CHUNK * 2 + tm * d * 6
                + (FFN_SKEW + 1) * 2 * nk * (tm + SUBLANES) * LANES * 4)
    return pl.pallas_call(
        functools.partial(_ffn_kernel, tm=tm),
        grid=(b, s // tm),
        in_specs=[
            _x_spec(tm, d),
            _const_spec((1, d)),
            _const_spec((1, d)),
            _layer_spec(w_up.shape, layer),
            _const_spec(cw.shape),
            _layer_spec(w_down.shape, layer),
        ],
        out_specs=_x_spec(tm, d),
        out_shape=jax.ShapeDtypeStruct(x.shape, x.dtype),
        scratch_shapes=[
            pltpu.VMEM((FFN_SKEW + 1, 2 * nk, 1, tm + SUBLANES, LANES), jnp.float32),
            pltpu.VMEM((nslab, 1, SUBLANES, LANES), jnp.float32),
            pltpu.VMEM((tm, d), jnp.bfloat16),
            pltpu.VMEM((2, tm, max(DOWN_GROUP, TAIL_GROUP) * FF_CHUNK), jnp.bfloat16),
            pltpu.VMEM((tm, d), jnp.float32),
        ],
        compiler_params=pltpu.CompilerParams(
            dimension_semantics=("arbitrary", "arbitrary"),
            vmem_limit_bytes=_vmem_limit(resident)),
        name="ffn",
    )(x, g_pre.reshape(1, d), g_post.reshape(1, d), w_up, cw, w_down)


def _odd_kernel(x_ref, gpre_ref, gpost_ref, win_ref, cw_ref, wout_ref, o_ref,
                ubuf, carry, h_ref, y_ref, f_ref, *, tm):
    i = pl.program_id(1)
    nk = FF_CHUNK // LANES

    @pl.when(i == 0)
    def _():
        carry[...] = jnp.zeros_like(carry)

    nch = SC_DIM // FF_CHUNK

    def up(c, ub, r0=0, rows=tm):
        col = c * FF_CHUNK
        s0 = c * nk
        if r0 == 0:
            ub[0:nk, :, 0:SUBLANES, :] = carry[s0:s0 + nk]
        h = h_ref[r0:r0 + rows, :]
        zb = jnp.dot(h, win_ref[:, col:col + FF_CHUNK], preferred_element_type=jnp.float32)
        zc = jnp.dot(h, win_ref[:, SC_DIM + col:SC_DIM + col + FF_CHUNK],
                     preferred_element_type=jnp.float32)
        zu = jnp.dot(h, win_ref[:, 2 * SC_DIM + col:2 * SC_DIM + col + FF_CHUNK],
                     preferred_element_type=jnp.float32)
        _to_slabs(ub.at[0:nk], SUBLANES + r0, zc * zu)
        _to_slabs(ub.at[nk:2 * nk], SUBLANES + r0, zb)
        if r0 + rows == tm:
            carry[s0:s0 + nk] = ub[0:nk, :, tm:tm + SUBLANES, :]

    def act(c, ub, r0=0, rows=tm):
        w = cw_ref[:, c * nk:(c + 1) * nk]
        for r in range(r0, r0 + rows, ROW_BLK):
            y = _conv3(ub.at[0:nk], r, ROW_BLK, w)
            b_gate = ub[nk:2 * nk, :, SUBLANES + r:SUBLANES + r + ROW_BLK, :]
            y_ref[r:r + ROW_BLK, c * FF_CHUNK:(c + 1) * FF_CHUNK] = _from_slabs(
                (b_gate * y).astype(jnp.bfloat16))

    _prenorm_rows(x_ref, gpre_ref, h_ref, 0, tm)
    for c in range(nch):
        ub = ubuf.at[c % 2]
        up(c, ub)
        act(c, ub)
    f_ref[...] = jnp.dot(y_ref[...], wout_ref[...], preferred_element_type=jnp.float32)
    _residual_rows(x_ref, f_ref, gpost_ref, o_ref, 0, tm)


def _odd_mixer(x, g_pre, g_post, w_in, conv_w, w_out):
    b, s, d = x.shape
    tm = SEQ_TILE
    nslab = SC_DIM // LANES
    nk = FF_CHUNK // LANES
    cw = conv_w.reshape(3, nslab, 1, 1, LANES)
    resident = (4 * tm * d * 4 + w_in.size * 2 + w_out.size * 2 + tm * SC_DIM * 2
                + tm * d * 6 + 4 * nk * (tm + SUBLANES) * LANES * 4)
    return pl.pallas_call(
        functools.partial(_odd_kernel, tm=tm),
        grid=(b, s // tm),
        in_specs=[
            _x_spec(tm, d),
            _const_spec((1, d)),
            _const_spec((1, d)),
            _const_spec(w_in.shape),
            _const_spec(cw.shape),
            _const_spec(w_out.shape),
        ],
        out_specs=_x_spec(tm, d),
        out_shape=jax.ShapeDtypeStruct(x.shape, x.dtype),
        scratch_shapes=[
            pltpu.VMEM((2, 2 * nk, 1, tm + SUBLANES, LANES), jnp.float32),
            pltpu.VMEM((nslab, 1, SUBLANES, LANES), jnp.float32),
            pltpu.VMEM((tm, d), jnp.bfloat16),
            pltpu.VMEM((tm, SC_DIM), jnp.bfloat16),
            pltpu.VMEM((tm, d), jnp.float32),
        ],
        compiler_params=pltpu.CompilerParams(
            dimension_semantics=("arbitrary", "arbitrary"),
            vmem_limit_bytes=_vmem_limit(resident)),
        name="odd_mixer",
    )(x, g_pre.reshape(1, d), g_post.reshape(1, d), w_in, cw, w_out)


def _split3_bf16(x):
    hi = x.astype(jnp.bfloat16)
    r = x - hi.astype(jnp.float32)
    mid = r.astype(jnp.bfloat16)
    lo = (r - mid.astype(jnp.float32)).astype(jnp.bfloat16)
    return hi, mid, lo


def _rope(x, c, s_up, s_dn):
    up = pltpu.roll(x, LANES - ROPE_HALF, axis=1)
    dn = pltpu.roll(x, ROPE_HALF, axis=1)
    return x * c + up * s_up + dn * s_dn


def _band_bias():
    r = np.arange(GROUP * WINDOW)[:, None] % WINDOW
    col = np.arange(2 * WINDOW)[None, :]
    band = (col > r) & (col <= r + WINDOW)
    first = band & (col >= WINDOW)
    return np.where(np.stack([band, first]), 0.0, -np.inf).astype(np.float32)


def _even_kernel(x_ref, cs_ref, e_ref, bias_ref, gpre_ref, gpost_ref, win_ref, cw_ref, cb_ref,
                 lng_ref, lnb_ref, sink_ref, wout_ref, o_ref,
                 abuf, h_ref, q_ref, kg_ref, vg_ref, mix_ref, f_ref, *, tm):
    i = pl.program_id(1)
    nslab = A_CH // LANES

    @pl.when(i == 0)
    def _():
        abuf[:, :, 0:A_HALO, :] = jnp.zeros((nslab, 1, A_HALO, LANES), jnp.float32)
        kg_ref[:, 0:WINDOW, :] = jnp.zeros((N_KV_HEADS, WINDOW, LANES), jnp.bfloat16)
        vg_ref[:, 0:WINDOW, 0:LANES] = jnp.zeros((N_KV_HEADS, WINDOW, LANES), jnp.bfloat16)
        vg_ref[:, :, LANES:2 * LANES] = jnp.ones((N_KV_HEADS, tm + WINDOW, LANES), jnp.bfloat16)

    @pl.when(i > 0)
    def _():
        abuf[:, :, 0:A_HALO, :] = abuf[:, :, tm:tm + A_HALO, :]
        kg_ref[:, 0:WINDOW, :] = kg_ref[:, tm:tm + WINDOW, :]
        vg_ref[:, 0:WINDOW, 0:LANES] = vg_ref[:, tm:tm + WINDOW, 0:LANES]

    def attend(j, g):
        qs = jnp.concatenate(
            [q_ref[j * WINDOW:(j + 1) * WINDOW, p * LANES:(p + 1) * LANES] for p in range(GROUP)],
            axis=0)
        kw = kg_ref[g, j * WINDOW:(j + 2) * WINDOW, :]
        vw = vg_ref[g, j * WINDOW:(j + 2) * WINDOW, :]
        sc = lax.dot_general(qs, kw, (((1,), (1,)), ((), ())),
                             preferred_element_type=jnp.float32)
        sc = sc + (bias_ref[jnp.where(i > 0, 0, 1)] if j == 0 else bias_ref[0])
        sink = sink_ref[g]
        m = jnp.maximum(jnp.max(sc, axis=-1, keepdims=True), sink)
        p_un = jnp.exp(sc - jnp.concatenate([m, m], axis=-1))
        pv = jnp.dot(p_un.astype(jnp.bfloat16), vw, preferred_element_type=jnp.float32)
        denom = pv[:, LANES:2 * LANES] + jnp.exp(sink - m)
        return pv[:, 0:LANES] * (1.0 / denom)

    def conv_ln(r0):
        acc = jnp.broadcast_to(cb_ref[...], (nslab, 1, CONV_ROWS, LANES))
        for tap in range(A_CONV):
            acc = acc + _rows(abuf, r0 + (A_HALO - A_CONV + 1) + tap, CONV_ROWS) * cw_ref[tap]
        y = _from_slabs(acc)
        mu = jnp.mean(y, axis=-1, keepdims=True)
        yc = y - mu
        y = yc * lax.rsqrt(jnp.mean(yc * yc, axis=-1, keepdims=True) + LN_EPS)
        y = y * lng_ref[...] + lnb_ref[...]
        mix_ref[r0:r0 + CONV_ROWS, 0:A_CH] = (y * jax.nn.sigmoid(y)).astype(jnp.bfloat16)

    rows = WINDOW

    def glu_in(j):
        r0 = j * rows
        _prenorm_rows(x_ref, gpre_ref, h_ref, r0, rows)
        za = jnp.dot(h_ref[r0:r0 + rows, :], win_ref[:, 0:2 * A_CH],
                     preferred_element_type=jnp.float32)
        _to_slabs(abuf, A_HALO + r0, za[:, 0:A_CH] * jax.nn.sigmoid(za[:, A_CH:2 * A_CH]))

    def qkv_in(j):
        r0 = j * rows
        t = None
        for piece in _split3_bf16(cs_ref[:, r0:r0 + rows]):
            d = lax.dot_general(piece, e_ref[...], (((0,), (0,)), ((), ())),
                                preferred_element_type=jnp.float32)
            t = d if t is None else t + d
        lane = lax.broadcasted_iota(jnp.int32, (rows, LANES), 1)
        c = t[:, 0:LANES] + jnp.where((lane & (HEAD_DIM - 1)) >= ROPE_DIM, 1.0, 0.0)
        s_up = t[:, LANES:2 * LANES]
        s_dn = t[:, 2 * LANES:3 * LANES]

        zq = jnp.dot(h_ref[r0:r0 + rows, :], win_ref[:, 2 * A_CH:EVEN_IN],
                     preferred_element_type=jnp.float32)
        for p in range(GROUP):
            q_ref[r0:r0 + rows, p * LANES:(p + 1) * LANES] = _rope(
                zq[:, p * LANES:(p + 1) * LANES], c, s_up, s_dn).astype(jnp.bfloat16)
        k = _rope(zq[:, Q_DIM:Q_DIM + KV_DIM], c, s_up, s_dn).astype(jnp.bfloat16)
        v = zq[:, Q_DIM + KV_DIM:Q_DIM + 2 * KV_DIM].astype(jnp.bfloat16)
        zero = jnp.zeros_like(k)
        for g in range(N_KV_HEADS):
            mine = (lane >= g * HEAD_DIM) & (lane < (g + 1) * HEAD_DIM)
            kg_ref[g, WINDOW + r0:WINDOW + r0 + rows, :] = jnp.where(mine, k, zero)
            vg_ref[g, WINDOW + r0:WINDOW + r0 + rows, 0:LANES] = jnp.where(mine, v, zero)

    def out(j):
        r0 = j * rows
        f_ref[r0:r0 + rows, :] = jnp.dot(mix_ref[r0:r0 + rows, :], wout_ref[...],
                                         preferred_element_type=jnp.float32)
        _residual_rows(x_ref, f_ref, gpost_ref, o_ref, r0, rows)

    nblk = tm // rows
    conv_per_blk = rows // CONV_ROWS
    for j in range(min(2, nblk)):
        glu_in(j)
        qkv_in(j)
    for j in range(nblk):
        o = attend(j, 0)
        if j + 2 < nblk:
            glu_in(j + 2)
        for cc in range(conv_per_blk // 2):
            conv_ln(j * rows + cc * CONV_ROWS)
        o = o + attend(j, 1)
        if j + 2 < nblk:
            qkv_in(j + 2)
        for cc in range(conv_per_blk // 2, conv_per_blk):
            conv_ln(j * rows + cc * CONV_ROWS)
        mix_ref[j * rows:(j + 1) * rows, A_CH:2 * A_CH] = jnp.concatenate(
            [o[p * WINDOW:(p + 1) * WINDOW] for p in range(GROUP)], axis=-1).astype(jnp.bfloat16)
        out(j)


def _pair_perm():
    idx = []
    for p in range(GROUP):
        for half in range(N_KV_HEADS):
            head = p + GROUP * half
            idx.extend(range(head * HEAD_DIM, (head + 1) * HEAD_DIM))
    return np.array(idx, dtype=np.int32)


def _even_mixer(x, cs, g_pre, g_post, w_in, conv_w, conv_b, ln_g, ln_b, sinks, w_out):
    b, s, d = x.shape
    tm = EVEN_TILE
    nslab = A_CH // LANES
    perm = _pair_perm()
    wq = w_in[:, 2 * A_CH:2 * A_CH + Q_DIM][:, perm] * (HEAD_DIM ** -0.5)
    w_in_p = jnp.concatenate([w_in[:, :2 * A_CH], wq, w_in[:, 2 * A_CH + Q_DIM:]],
                             axis=1).astype(jnp.bfloat16)
    w_out_p = jnp.concatenate([w_out[:A_CH], w_out[A_CH:][perm]], axis=0).astype(jnp.bfloat16)
    cw = conv_w.reshape(A_CONV, nslab, 1, 1, LANES)
    cb = conv_b.reshape(nslab, 1, 1, LANES)
    sink_rows = jnp.broadcast_to(
        jnp.repeat(sinks.reshape(N_KV_HEADS, GROUP), WINDOW, axis=1)[..., None],
        (N_KV_HEADS, GROUP * WINDOW, LANES))
    e = jnp.asarray(_rope_expand_matrix(), jnp.bfloat16)
    bias = jnp.asarray(_band_bias())
    steps = s // tm
    cs_spec = pl.BlockSpec((ROPE_DIM, tm), lambda bb, i: (0, bb * steps + i))
    resident = (4 * tm * d * 4 + 2 * ROPE_DIM * tm * 4 + w_in_p.size * 2 + w_out_p.size * 2
                + bias.size * 4 + nslab * (tm + A_HALO) * LANES * 4 + tm * d * 6 + tm * Q_DIM * 2
                + N_KV_HEADS * (tm + WINDOW) * 3 * LANES * 2 + tm * d * 2)
    return pl.pallas_call(
        functools.partial(_even_kernel, tm=tm),
        grid=(b, steps),
        in_specs=[
            _x_spec(tm, d), cs_spec,
            _const_spec(e.shape),
            _const_spec(bias.shape),
            _const_spec((1, d)),
            _const_spec((1, d)),
            _const_spec(w_in_p.shape),
            _const_spec(cw.shape),
            _const_spec(cb.shape),
            _const_spec((1, A_CH)),
            _const_spec((1, A_CH)),
            _const_spec(sink_rows.shape),
            _const_spec(w_out_p.shape),
        ],
        out_specs=_x_spec(tm, d),
        out_shape=jax.ShapeDtypeStruct(x.shape, x.dtype),
        scratch_shapes=[
            pltpu.VMEM((nslab, 1, tm + A_HALO, LANES), jnp.float32),
            pltpu.VMEM((tm, d), jnp.bfloat16),
            pltpu.VMEM((tm, Q_DIM), jnp.bfloat16),
            pltpu.VMEM((N_KV_HEADS, tm + WINDOW, LANES), jnp.bfloat16),
            pltpu.VMEM((N_KV_HEADS, tm + WINDOW, 2 * LANES), jnp.bfloat16),
            pltpu.VMEM((tm, 2 * A_CH), jnp.bfloat16),
            pltpu.VMEM((tm, d), jnp.float32),
        ],
        compiler_params=pltpu.CompilerParams(
            dimension_semantics=("arbitrary", "arbitrary"),
            vmem_limit_bytes=_vmem_limit(resident)),
        name="even_mixer",
    )(x, cs, e, bias, g_pre.reshape(1, d), g_post.reshape(1, d), w_in_p, cw, cb,
      ln_g.reshape(1, A_CH), ln_b.reshape(1, A_CH), sink_rows, w_out_p)


def kernel(x, positions, mix_norm_pre, mix_norm_post, ffn_norm_pre, ffn_norm_post, ev_w_in, ev_a_conv_w, ev_a_conv_b, ev_a_ln_g, ev_a_ln_b, ev_sinks, ev_w_out, od_w_in, od_conv_w, od_w_out, ffn_w_up, ffn_conv_w, ffn_w_down):
    depth = mix_norm_pre.shape[0]
    d = x.shape[2]
    assert all(x.shape[1] % t == 0 for t in (SEQ_TILE, EVEN_TILE, FFN_TILE))
    assert EVEN_TILE % WINDOW == 0
    cs = _rope_tables(positions)
    bf = jnp.bfloat16
    w_up = ffn_w_up.astype(bf)
    w_down = ffn_w_down.astype(bf)
    for i in range(depth):
        j = i // 2
        if i % 2 == 0:
            x = _even_mixer(x, cs, mix_norm_pre[i], mix_norm_post[i], ev_w_in[j],
                            ev_a_conv_w[j], ev_a_conv_b[j], ev_a_ln_g[j], ev_a_ln_b[j],
                            ev_sinks[j], ev_w_out[j])
        else:
            x = _odd_mixer(x, mix_norm_pre[i], mix_norm_post[i], od_w_in[j].astype(bf),
                           od_conv_w[j], od_w_out[j].astype(bf))
        x = _ffn(x, i, ffn_norm_pre[i], ffn_norm_post[i], w_up, ffn_conv_w[i], w_down)
    return x
```

```python
import functools

import jax
import jax.numpy as jnp
import numpy as np
from jax import lax
from jax.experimental import pallas as pl
from jax.experimental.pallas import tpu as pltpu

D_MODEL = 1024
HEAD_DIM = 64
A_CH = 512
A_CONV = 31
N_Q_HEADS = 8
N_KV_HEADS = 2
GROUP = 4
WINDOW = 128
ROPE_THETA = 500000.0
ROPE_DIM = 16
ROPE_HALF = ROPE_DIM // 2
Q_DIM = 512
KV_DIM = 128
EVEN_IN = 1792
SC_DIM = 1024
D_FF = 2816
RMS_EPS = 1e-6
LN_EPS = 1e-5
LOG2E = 1.4426950408889634

LANES = 128
SUBLANES = 8
VMEM_BYTES = 64 << 20

SEQ_TILE = 1024
FFN_TILE = 512
EVEN_TILE = 1024
FF_CHUNK = 256
CONV_ROWS = 64
OUT_BLKS = 2
ROW_BLK = 64
EDGE_BLK = 128
FFN_SKEW = 3
DOWN_GROUP = 2
TAIL_GROUP = 1
TAIL_BLK = 128
A_HALO = 32


def _rows(ref, start, size):
    return ref[:, pl.ds(0, 1, stride=2), pl.ds(start, size), :]


def _rms(x, g):
    ms = jnp.mean(x * x, axis=-1, keepdims=True)
    return x * lax.rsqrt(ms + RMS_EPS) * g


def _to_slabs(ref, row0, val):
    rows = val.shape[0]
    for k in range(val.shape[1] // LANES):
        ref[k, 0, row0:row0 + rows, :] = val[:, k * LANES:(k + 1) * LANES]


def _from_slabs(val):
    return jnp.concatenate([val[k, 0] for k in range(val.shape[0])], axis=-1)


def _conv3(ubuf, r0, rows, w):
    u0 = ubuf[:, :, SUBLANES + r0:SUBLANES + r0 + rows, :]
    u1 = _rows(ubuf, SUBLANES - 1 + r0, rows)
    u2 = _rows(ubuf, SUBLANES - 2 + r0, rows)
    return w[2] * u0 + w[1] * u1 + w[0] * u2


def _prenorm_rows(x_ref, g_ref, h_ref, r0, rows):
    g = g_ref[...]
    for r in range(r0, r0 + rows, ROW_BLK):
        h_ref[r:r + ROW_BLK, :] = _rms(x_ref[r:r + ROW_BLK, :], g).astype(jnp.bfloat16)


def _residual_rows(x_ref, f_ref, g_ref, o_ref, r0, rows):
    g = g_ref[...]
    for r in range(r0, r0 + rows, ROW_BLK):
        o_ref[r:r + ROW_BLK, :] = x_ref[r:r + ROW_BLK, :] + _rms(f_ref[r:r + ROW_BLK, :], g)


def _const_spec(shape):
    nd = len(shape)
    return pl.BlockSpec(shape, lambda b, i: (0,) * nd, pipeline_mode=pl.Buffered(1))


def _layer_spec(shape, layer):
    nd = len(shape)
    return pl.BlockSpec((None,) + tuple(shape[1:]), lambda b, i: (layer,) + (0,) * (nd - 1),
                        pipeline_mode=pl.Buffered(1))


def _x_spec(tm, width):
    return pl.BlockSpec((None, tm, width), lambda b, i: (b, i, 0))


def _vmem_limit(resident_bytes):
    return int(min(VMEM_BYTES - (4 << 20), 2 * resident_bytes + (8 << 20)))


def _rope_kernel(pos_ref, invf_ref, cs_ref):
    ang = invf_ref[...] * pos_ref[...]
    cs_ref[0:ROPE_HALF, :] = jnp.cos(ang)
    cs_ref[ROPE_HALF:ROPE_DIM, :] = jnp.sin(ang)


def _rope_tables(positions):
    inv_freq = ROPE_THETA ** (-(jnp.arange(ROPE_HALF, dtype=jnp.float32) * 2.0 / ROPE_DIM))
    pos = positions.astype(jnp.float32).reshape(1, -1)
    return pl.pallas_call(
        _rope_kernel,
        out_shape=jax.ShapeDtypeStruct((ROPE_DIM, pos.shape[1]), jnp.float32),
        name="rope_tables",
    )(pos, inv_freq.reshape(ROPE_HALF, 1))


def _rope_expand_matrix():
    e = np.zeros((ROPE_DIM, 3 * LANES), np.float32)
    for lane in range(LANES):
        d = lane % HEAD_DIM
        if d < ROPE_HALF:
            e[d, lane] = 1.0
            e[ROPE_HALF + d, LANES + lane] = -1.0
        elif d < ROPE_DIM:
            e[d - ROPE_HALF, lane] = 1.0
            e[d, 2 * LANES + lane] = 1.0
    return e


def _down_groups(nch):
    cuts = list(range(0, nch - TAIL_GROUP + 1, DOWN_GROUP)) + [nch]
    groups = list(zip(cuts[:-1], cuts[1:]))
    assert groups[-1][1] - groups[-1][0] == TAIL_GROUP
    return groups


def _ffn_kernel(x_ref, gpre_ref, gpost_ref, wup_ref, cw_ref, wdown_ref, o_ref,
                ubuf, carry, h_ref, act_ref, f_ref, *, tm):
    i = pl.program_id(1)
    nk = FF_CHUNK // LANES
    nslab_half = D_FF // LANES
    nch = D_FF // FF_CHUNK
    groups = _down_groups(nch)
    group_of = [g for g, (c0, c1) in enumerate(groups) for _ in range(c0, c1)]

    @pl.when(i == 0)
    def _():
        carry[...] = jnp.zeros_like(carry)

    def up(c, ub, r0=0, rows=tm):
        sg = c * nk
        sv = nslab_half + c * nk
        if r0 == 0:
            ub[0:nk, :, 0:SUBLANES, :] = carry[sg:sg + nk]
            ub[nk:2 * nk, :, 0:SUBLANES, :] = carry[sv:sv + nk]
        h = h_ref[r0:r0 + rows, :]
        ug = jnp.dot(h, wup_ref[:, c * FF_CHUNK:(c + 1) * FF_CHUNK],
                     preferred_element_type=jnp.float32)
        uv = jnp.dot(h, wup_ref[:, D_FF + c * FF_CHUNK:D_FF + (c + 1) * FF_CHUNK],
                     preferred_element_type=jnp.float32)
        _to_slabs(ub.at[0:nk], SUBLANES + r0, ug)
        _to_slabs(ub.at[nk:2 * nk], SUBLANES + r0, uv)
        if r0 + rows == tm:
            carry[sg:sg + nk] = ub[0:nk, :, tm:tm + SUBLANES, :]
            carry[sv:sv + nk] = ub[nk:2 * nk, :, tm:tm + SUBLANES, :]

    def act(c, ub, r0=0, rows=tm):
        w = jnp.concatenate([cw_ref[:, c * nk:(c + 1) * nk],
                             cw_ref[:, nslab_half + c * nk:nslab_half + (c + 1) * nk]], axis=1)
        for r in range(r0, r0 + rows, ROW_BLK):
            y = _conv3(ub, r, ROW_BLK, w)
            yg = y[0:nk].astype(jnp.bfloat16)
            a = yg * jax.nn.sigmoid(yg) * y[nk:2 * nk].astype(jnp.bfloat16)
            g = group_of[c]
            k = c - groups[g][0]
            act_ref[g % 2, r:r + ROW_BLK, k * FF_CHUNK:(k + 1) * FF_CHUNK] = _from_slabs(a)

    def down(g, r0=0, rows=tm):
        c0, c1 = groups[g]
        part = jnp.dot(act_ref[g % 2, r0:r0 + rows, 0:(c1 - c0) * FF_CHUNK],
                       wdown_ref[c0 * FF_CHUNK:c1 * FF_CHUNK, :],
                       preferred_element_type=jnp.float32)
        if c0 == 0:
            f_ref[r0:r0 + rows, :] = part
        else:
            f_ref[r0:r0 + rows, :] += part

    nbuf = FFN_SKEW + 1
    bufs = [ubuf.at[k] for k in range(nbuf)]
    for r0 in range(0, tm, EDGE_BLK):
        _prenorm_rows(x_ref, gpre_ref, h_ref, r0, EDGE_BLK)
        up(0, bufs[0], r0, EDGE_BLK)
    for c in range(1, FFN_SKEW):
        up(c, bufs[c])
    last = nch - 1
    for c in range(last):
        if c + FFN_SKEW < nch:
            up(c + FFN_SKEW, bufs[(c + FFN_SKEW) % nbuf])
        act(c, bufs[c % nbuf])
        g = group_of[c]
        if c + 1 == groups[g][1]:
            down(g)
    for r0 in range(0, tm, TAIL_BLK):
        act(last, bufs[last % nbuf], r0, TAIL_BLK)
        down(len(groups) - 1, r0, TAIL_BLK)
        _residual_rows(x_ref, f_ref, gpost_ref, o_ref, r0, TAIL_BLK)


def _ffn(x, layer, g_pre, g_post, w_up, conv_w, w_down):
    b, s, d = x.shape
    tm = FFN_TILE
    nslab = 2 * D_FF // LANES
    nk = FF_CHUNK // LANES
    cw = conv_w.reshape(3, nslab, 1, 1, LANES)
    resident = (4 * tm * d * 4 + 2 * d * D_FF * 2 + D_FF * d * 2
                + 2 * tm * max(DOWN_GROUP, TAIL_GROUP) * FF_CHUNK * 2 + tm * d * 6
                + (FFN_SKEW + 1) * 2 * nk * (tm + SUBLANES) * LANES * 4)
    return pl.pallas_call(
        functools.partial(_ffn_kernel, tm=tm),
        grid=(b, s // tm),
        in_specs=[
            _x_spec(tm, d),
            _const_spec((1, d)),
            _const_spec((1, d)),
            _layer_spec(w_up.shape, layer),
            _const_spec(cw.shape),
            _layer_spec(w_down.shape, layer),
        ],
        out_specs=_x_spec(tm, d),
        out_shape=jax.ShapeDtypeStruct(x.shape, x.dtype),
        scratch_shapes=[
            pltpu.VMEM((FFN_SKEW + 1, 2 * nk, 1, tm + SUBLANES, LANES), jnp.float32),
            pltpu.VMEM((nslab, 1, SUBLANES, LANES), jnp.float32),
            pltpu.VMEM((tm, d), jnp.bfloat16),
            pltpu.VMEM((2, tm, max(DOWN_GROUP, TAIL_GROUP) * FF_CHUNK), jnp.bfloat16),
            pltpu.VMEM((tm, d), jnp.float32),
        ],
        compiler_params=pltpu.CompilerParams(
            dimension_semantics=("arbitrary", "arbitrary"),
            vmem_limit_bytes=_vmem_limit(resident)),
        name="ffn",
    )(x, g_pre.reshape(1, d), g_post.reshape(1, d), w_up, cw, w_down)


def _odd_kernel(x_ref, gpre_ref, gpost_ref, win_ref, cw_ref, wout_ref, o_ref,
                ubuf, carry, h_ref, y_ref, f_ref, *, tm):
    i = pl.program_id(1)
    nk = FF_CHUNK // LANES

    @pl.when(i == 0)
    def _():
        carry[...] = jnp.zeros_like(carry)

    nch = SC_DIM // FF_CHUNK

    def up(c, ub, r0=0, rows=tm):
        col = c * FF_CHUNK
        s0 = c * nk
        if r0 == 0:
            ub[0:nk, :, 0:SUBLANES, :] = carry[s0:s0 + nk]
        h = h_ref[r0:r0 + rows, :]
        zb = jnp.dot(h, win_ref[:, col:col + FF_CHUNK], preferred_element_type=jnp.float32)
        zc = jnp.dot(h, win_ref[:, SC_DIM + col:SC_DIM + col + FF_CHUNK],
                     preferred_element_type=jnp.float32)
        zu = jnp.dot(h, win_ref[:, 2 * SC_DIM + col:2 * SC_DIM + col + FF_CHUNK],
                     preferred_element_type=jnp.float32)
        _to_slabs(ub.at[0:nk], SUBLANES + r0, zc * zu)
        _to_slabs(ub.at[nk:2 * nk], SUBLANES + r0, zb)
        if r0 + rows == tm:
            carry[s0:s0 + nk] = ub[0:nk, :, tm:tm + SUBLANES, :]

    def act(c, ub, r0=0, rows=tm):
        w = cw_ref[:, c * nk:(c + 1) * nk]
        for r in range(r0, r0 + rows, ROW_BLK):
            y = _conv3(ub.at[0:nk], r, ROW_BLK, w)
            b_gate = ub[nk:2 * nk, :, SUBLANES + r:SUBLANES + r + ROW_BLK, :]
            y_ref[r:r + ROW_BLK, c * FF_CHUNK:(c + 1) * FF_CHUNK] = _from_slabs(
                (b_gate * y).astype(jnp.bfloat16))

    _prenorm_rows(x_ref, gpre_ref, h_ref, 0, tm)
    for c in range(nch):
        ub = ubuf.at[c % 2]
        up(c, ub)
        act(c, ub)
    f_ref[...] = jnp.dot(y_ref[...], wout_ref[...], preferred_element_type=jnp.float32)
    _residual_rows(x_ref, f_ref, gpost_ref, o_ref, 0, tm)


def _odd_mixer(x, g_pre, g_post, w_in, conv_w, w_out):
    b, s, d = x.shape
    tm = SEQ_TILE
    nslab = SC_DIM // LANES
    nk = FF_CHUNK // LANES
    cw = conv_w.reshape(3, nslab, 1, 1, LANES)
    resident = (4 * tm * d * 4 + w_in.size * 2 + w_out.size * 2 + tm * SC_DIM * 2
                + tm * d * 6 + 4 * nk * (tm + SUBLANES) * LANES * 4)
    return pl.pallas_call(
        functools.partial(_odd_kernel, tm=tm),
        grid=(b, s // tm),
        in_specs=[
            _x_spec(tm, d),
            _const_spec((1, d)),
            _const_spec((1, d)),
            _const_spec(w_in.shape),
            _const_spec(cw.shape),
            _const_spec(w_out.shape),
        ],
        out_specs=_x_spec(tm, d),
        out_shape=jax.ShapeDtypeStruct(x.shape, x.dtype),
        scratch_shapes=[
            pltpu.VMEM((2, 2 * nk, 1, tm + SUBLANES, LANES), jnp.float32),
            pltpu.VMEM((nslab, 1, SUBLANES, LANES), jnp.float32),
            pltpu.VMEM((tm, d), jnp.bfloat16),
            pltpu.VMEM((tm, SC_DIM), jnp.bfloat16),
            pltpu.VMEM((tm, d), jnp.float32),
        ],
        compiler_params=pltpu.CompilerParams(
            dimension_semantics=("arbitrary", "arbitrary"),
            vmem_limit_bytes=_vmem_limit(resident)),
        name="odd_mixer",
    )(x, g_pre.reshape(1, d), g_post.reshape(1, d), w_in, cw, w_out)


def _split3_bf16(x):
    hi = x.astype(jnp.bfloat16)
    r = x - hi.astype(jnp.float32)
    mid = r.astype(jnp.bfloat16)
    lo = (r - mid.astype(jnp.float32)).astype(jnp.bfloat16)
    return hi, mid, lo


def _rope(x, c, s_up, s_dn):
    up = pltpu.roll(x, LANES - ROPE_HALF, axis=1)
    dn = pltpu.roll(x, ROPE_HALF, axis=1)
    return x * c + up * s_up + dn * s_dn


def _band_bias():
    r = np.arange(GROUP * WINDOW)[:, None] % WINDOW
    col = np.arange(2 * WINDOW)[None, :]
    band = (col > r) & (col <= r + WINDOW)
    first = band & (col >= WINDOW)
    return np.where(np.stack([band, first]), 0.0, -np.inf).astype(np.float32)


def _even_kernel(x_ref, cs_ref, e_ref, bias_ref, gpre_ref, gpost_ref, win_ref, cw_ref, cb_ref,
                 lng_ref, lnb_ref, sink_ref, wout_ref, o_ref,
                 abuf, h_ref, q_ref, kg_ref, vg_ref, mix_ref, f_ref, *, tm):
    i = pl.program_id(1)
    nslab = A_CH // LANES

    @pl.when(i == 0)
    def _():
        abuf[:, :, 0:A_HALO, :] = jnp.zeros((nslab, 1, A_HALO, LANES), jnp.float32)
        kg_ref[:, 0:WINDOW, :] = jnp.zeros((N_KV_HEADS, WINDOW, LANES), jnp.bfloat16)
        vg_ref[:, 0:WINDOW, 0:LANES] = jnp.zeros((N_KV_HEADS, WINDOW, LANES), jnp.bfloat16)
        vg_ref[:, :, LANES:2 * LANES] = jnp.ones((N_KV_HEADS, tm + WINDOW, LANES), jnp.bfloat16)

    @pl.when(i > 0)
    def _():
        abuf[:, :, 0:A_HALO, :] = abuf[:, :, tm:tm + A_HALO, :]
        kg_ref[:, 0:WINDOW, :] = kg_ref[:, tm:tm + WINDOW, :]
        vg_ref[:, 0:WINDOW, 0:LANES] = vg_ref[:, tm:tm + WINDOW, 0:LANES]

    def attend(j, g):
        qs = jnp.concatenate(
            [q_ref[j * WINDOW:(j + 1) * WINDOW, p * LANES:(p + 1) * LANES] for p in range(GROUP)],
            axis=0)
        kw = kg_ref[g, j * WINDOW:(j + 2) * WINDOW, :]
        vw = vg_ref[g, j * WINDOW:(j + 2) * WINDOW, :]
        sc = lax.dot_general(qs, kw, (((1,), (1,)), ((), ())),
                             preferred_element_type=jnp.float32)
        sc = sc + (bias_ref[jnp.where(i > 0, 0, 1)] if j == 0 else bias_ref[0])
        sink = sink_ref[g]
        m = jnp.maximum(jnp.max(sc, axis=-1, keepdims=True), sink)
        p_un = jnp.exp2(sc - jnp.concatenate([m, m], axis=-1))
        pv = jnp.dot(p_un.astype(jnp.bfloat16), vw, preferred_element_type=jnp.float32)
        denom = pv[:, LANES:2 * LANES] + jnp.exp2(sink - m)
        return pv[:, 0:LANES] * (1.0 / denom)

    def conv_ln(r0):
        acc = jnp.broadcast_to(cb_ref[...], (nslab, 1, CONV_ROWS, LANES))
        for tap in range(A_CONV):
            acc = acc + _rows(abuf, r0 + (A_HALO - A_CONV + 1) + tap, CONV_ROWS) * cw_ref[tap]
        y = _from_slabs(acc)
        mu = jnp.mean(y, axis=-1, keepdims=True)
        yc = y - mu
        y = yc * lax.rsqrt(jnp.mean(yc * yc, axis=-1, keepdims=True) + LN_EPS)
        y = y * lng_ref[...] + lnb_ref[...]
        mix_ref[r0:r0 + CONV_ROWS, 0:A_CH] = (y * jax.nn.sigmoid(y)).astype(jnp.bfloat16)

    rows = WINDOW

    def glu_in(j):
        r0 = j * rows
        _prenorm_rows(x_ref, gpre_ref, h_ref, r0, rows)
        za = jnp.dot(h_ref[r0:r0 + rows, :], win_ref[:, 0:2 * A_CH],
                     preferred_element_type=jnp.float32)
        _to_slabs(abuf, A_HALO + r0, za[:, 0:A_CH] * jax.nn.sigmoid(za[:, A_CH:2 * A_CH]))

    def qkv_in(j):
        r0 = j * rows
        t = None
        for piece in _split3_bf16(cs_ref[:, r0:r0 + rows]):
            d = lax.dot_general(piece, e_ref[...], (((0,), (0,)), ((), ())),
                                preferred_element_type=jnp.float32)
            t = d if t is None else t + d
        lane = lax.broadcasted_iota(jnp.int32, (rows, LANES), 1)
        c = t[:, 0:LANES] + jnp.where((lane & (HEAD_DIM - 1)) >= ROPE_DIM, 1.0, 0.0)
        s_up = t[:, LANES:2 * LANES]
        s_dn = t[:, 2 * LANES:3 * LANES]

        zq = jnp.dot(h_ref[r0:r0 + rows, :], win_ref[:, 2 * A_CH:EVEN_IN],
                     preferred_element_type=jnp.float32)
        cq, squ, sqd = c * LOG2E, s_up * LOG2E, s_dn * LOG2E
        for p in range(GROUP):
            q_ref[r0:r0 + rows, p * LANES:(p + 1) * LANES] = _rope(
                zq[:, p * LANES:(p + 1) * LANES], cq, squ, sqd).astype(jnp.bfloat16)
        k = _rope(zq[:, Q_DIM:Q_DIM + KV_DIM], c, s_up, s_dn).astype(jnp.bfloat16)
        v = zq[:, Q_DIM + KV_DIM:Q_DIM + 2 * KV_DIM].astype(jnp.bfloat16)
        zero = jnp.zeros_like(k)
        for g in range(N_KV_HEADS):
            mine = (lane >= g * HEAD_DIM) & (lane < (g + 1) * HEAD_DIM)
            kg_ref[g, WINDOW + r0:WINDOW + r0 + rows, :] = jnp.where(mine, k, zero)
            vg_ref[g, WINDOW + r0:WINDOW + r0 + rows, 0:LANES] = jnp.where(mine, v, zero)

    def out(j, nb):
        r0 = j * rows
        f_ref[r0:r0 + nb * rows, :] = jnp.dot(mix_ref[r0:r0 + nb * rows, :], wout_ref[...],
                                              preferred_element_type=jnp.float32)
        _residual_rows(x_ref, f_ref, gpost_ref, o_ref, r0, nb * rows)

    nblk = tm // rows
    conv_per_blk = rows // CONV_ROWS
    for j in range(min(2, nblk)):
        glu_in(j)
        qkv_in(j)
    for j in range(nblk):
        o = attend(j, 0)
        if j + 2 < nblk:
            glu_in(j + 2)
        for cc in range(conv_per_blk // 2):
            conv_ln(j * rows + cc * CONV_ROWS)
        o = o + attend(j, 1)
        if j + 2 < nblk:
            qkv_in(j + 2)
        for cc in range(conv_per_blk // 2, conv_per_blk):
            conv_ln(j * rows + cc * CONV_ROWS)
        mix_ref[j * rows:(j + 1) * rows, A_CH:2 * A_CH] = jnp.concatenate(
            [o[p * WINDOW:(p + 1) * WINDOW] for p in range(GROUP)], axis=-1).astype(jnp.bfloat16)
        if j % OUT_BLKS == OUT_BLKS - 1:
            out(j + 1 - OUT_BLKS, OUT_BLKS)
    assert nblk % OUT_BLKS == 0


def _pair_perm():
    idx = []
    for p in range(GROUP):
        for half in range(N_KV_HEADS):
            head = p + GROUP * half
            idx.extend(range(head * HEAD_DIM, (head + 1) * HEAD_DIM))
    return np.array(idx, dtype=np.int32)


def _even_mixer(x, cs, g_pre, g_post, w_in, conv_w, conv_b, ln_g, ln_b, sinks, w_out):
    b, s, d = x.shape
    tm = EVEN_TILE
    nslab = A_CH // LANES
    perm = _pair_perm()
    wq = w_in[:, 2 * A_CH:2 * A_CH + Q_DIM][:, perm] * (HEAD_DIM ** -0.5)
    w_in_p = jnp.concatenate([w_in[:, :2 * A_CH], wq, w_in[:, 2 * A_CH + Q_DIM:]],
                             axis=1).astype(jnp.bfloat16)
    w_out_p = jnp.concatenate([w_out[:A_CH], w_out[A_CH:][perm]], axis=0).astype(jnp.bfloat16)
    cw = conv_w.reshape(A_CONV, nslab, 1, 1, LANES)
    cb = conv_b.reshape(nslab, 1, 1, LANES)
    sink_rows = jnp.broadcast_to(
        jnp.repeat(sinks.reshape(N_KV_HEADS, GROUP) * LOG2E, WINDOW, axis=1)[..., None],
        (N_KV_HEADS, GROUP * WINDOW, LANES))
    e = jnp.asarray(_rope_expand_matrix(), jnp.bfloat16)
    bias = jnp.asarray(_band_bias())
    steps = s // tm
    cs_spec = pl.BlockSpec((ROPE_DIM, tm), lambda bb, i: (0, bb * steps + i))
    resident = (4 * tm * d * 4 + 2 * ROPE_DIM * tm * 4 + w_in_p.size * 2 + w_out_p.size * 2
                + bias.size * 4 + nslab * (tm + A_HALO) * LANES * 4 + tm * d * 6 + tm * Q_DIM * 2
                + N_KV_HEADS * (tm + WINDOW) * 3 * LANES * 2 + tm * d * 2)
    return pl.pallas_call(
        functools.partial(_even_kernel, tm=tm),
        grid=(b, steps),
        in_specs=[
            _x_spec(tm, d), cs_spec,
            _const_spec(e.shape),
            _const_spec(bias.shape),
            _const_spec((1, d)),
            _const_spec((1, d)),
            _const_spec(w_in_p.shape),
            _const_spec(cw.shape),
            _const_spec(cb.shape),
            _const_spec((1, A_CH)),
            _const_spec((1, A_CH)),
            _const_spec(sink_rows.shape),
            _const_spec(w_out_p.shape),
        ],
        out_specs=_x_spec(tm, d),
        out_shape=jax.ShapeDtypeStruct(x.shape, x.dtype),
        scratch_shapes=[
            pltpu.VMEM((nslab, 1, tm + A_HALO, LANES), jnp.float32),
            pltpu.VMEM((tm, d), jnp.bfloat16),
            pltpu.VMEM((tm, Q_DIM), jnp.bfloat16),
            pltpu.VMEM((N_KV_HEADS, tm + WINDOW, LANES), jnp.bfloat16),
            pltpu.VMEM((N_KV_HEADS, tm + WINDOW, 2 * LANES), jnp.bfloat16),
            pltpu.VMEM((tm, 2 * A_CH), jnp.bfloat16),
            pltpu.VMEM((tm, d), jnp.float32),
        ],
        compiler_params=pltpu.CompilerParams(
            dimension_semantics=("arbitrary", "arbitrary"),
            vmem_limit_bytes=_vmem_limit(resident)),
        name="even_mixer",
    )(x, cs, e, bias, g_pre.reshape(1, d), g_post.reshape(1, d), w_in_p, cw, cb,
      ln_g.reshape(1, A_CH), ln_b.reshape(1, A_CH), sink_rows, w_out_p)


def kernel(x, positions, mix_norm_pre, mix_norm_post, ffn_norm_pre, ffn_norm_post, ev_w_in, ev_a_conv_w, ev_a_conv_b, ev_a_ln_g, ev_a_ln_b, ev_sinks, ev_w_out, od_w_in, od_conv_w, od_w_out, ffn_w_up, ffn_conv_w, ffn_w_down):
    depth = mix_norm_pre.shape[0]
    d = x.shape[2]
    assert all(x.shape[1] % t == 0 for t in (SEQ_TILE, EVEN_TILE, FFN_TILE))
    assert EVEN_TILE % WINDOW == 0
    cs = _rope_tables(positions)
    bf = jnp.bfloat16
    w_up = ffn_w_up.astype(bf)
    w_down = ffn_w_down.astype(bf)
    for i in range(depth):
        j = i // 2
        if i % 2 == 0:
            x = _even_mixer(x, cs, mix_norm_pre[i], mix_norm_post[i], ev_w_in[j],
                            ev_a_conv_w[j], ev_a_conv_b[j], ev_a_ln_g[j], ev_a_ln_b[j],
                            ev_sinks[j], ev_w_out[j])
        else:
            x = _odd_mixer(x, mix_norm_pre[i], mix_norm_post[i], od_w_in[j].astype(bf),
                           od_conv_w[j], od_w_out[j].astype(bf))
        x = _ffn(x, i, ffn_norm_pre[i], ffn_norm_post[i], w_up, ffn_conv_w[i], w_down)
    return x
```

```python
import functools

import jax
import jax.numpy as jnp
import numpy as np
from jax import lax
from jax.experimental import pallas as pl
from jax.experimental.pallas import tpu as pltpu

D_MODEL = 1024
HEAD_DIM = 64
A_CH = 512
A_CONV = 31
N_Q_HEADS = 8
N_KV_HEADS = 2
GROUP = 4
WINDOW = 128
ROPE_THETA = 500000.0
ROPE_DIM = 16
ROPE_HALF = ROPE_DIM // 2
Q_DIM = 512
KV_DIM = 128
EVEN_IN = 1792
SC_DIM = 1024
D_FF = 2816
RMS_EPS = 1e-6
LN_EPS = 1e-5
LOG2E = 1.4426950408889634

LANES = 128
SUBLANES = 8
VMEM_BYTES = 64 << 20

SEQ_TILE = 1024
FFN_TILE = 512
EVEN_TILE = 1024
FF_CHUNK = 256
CONV_ROWS = 64
OUT_BLKS = 2
ROW_BLK = 64
EDGE_BLK = 256
FFN_SKEW = 3
DOWN_GROUP = 2
TAIL_GROUP = 1
TAIL_BLK = 256
A_HALO = 32


def _rows(ref, start, size):
    return ref[:, pl.ds(0, 1, stride=2), pl.ds(start, size), :]


def _rms(x, g):
    ms = jnp.mean(x * x, axis=-1, keepdims=True)
    return x * lax.rsqrt(ms + RMS_EPS) * g


def _to_slabs(ref, row0, val):
    rows = val.shape[0]
    for k in range(val.shape[1] // LANES):
        ref[k, 0, row0:row0 + rows, :] = val[:, k * LANES:(k + 1) * LANES]


def _from_slabs(val):
    return jnp.concatenate([val[k, 0] for k in range(val.shape[0])], axis=-1)


def _conv3(ubuf, r0, rows, w):
    u0 = ubuf[:, :, SUBLANES + r0:SUBLANES + r0 + rows, :]
    u1 = _rows(ubuf, SUBLANES - 1 + r0, rows)
    u2 = _rows(ubuf, SUBLANES - 2 + r0, rows)
    return w[2] * u0 + w[1] * u1 + w[0] * u2


def _prenorm_rows(x_ref, g_ref, h_ref, r0, rows):
    g = g_ref[...]
    for r in range(r0, r0 + rows, ROW_BLK):
        h_ref[r:r + ROW_BLK, :] = _rms(x_ref[r:r + ROW_BLK, :], g).astype(jnp.bfloat16)


def _residual_rows(x_ref, f_ref, g_ref, o_ref, r0, rows):
    g = g_ref[...]
    for r in range(r0, r0 + rows, ROW_BLK):
        o_ref[r:r + ROW_BLK, :] = x_ref[r:r + ROW_BLK, :] + _rms(f_ref[r:r + ROW_BLK, :], g)


def _const_spec(shape):
    nd = len(shape)
    return pl.BlockSpec(shape, lambda b, i: (0,) * nd, pipeline_mode=pl.Buffered(1))


def _layer_spec(shape, layer):
    nd = len(shape)
    return pl.BlockSpec((None,) + tuple(shape[1:]), lambda b, i: (layer,) + (0,) * (nd - 1),
                        pipeline_mode=pl.Buffered(1))


def _x_spec(tm, width):
    return pl.BlockSpec((None, tm, width), lambda b, i: (b, i, 0))


def _vmem_limit(resident_bytes):
    return int(min(VMEM_BYTES - (4 << 20), 2 * resident_bytes + (8 << 20)))


def _rope_kernel(pos_ref, invf_ref, cs_ref):
    ang = invf_ref[...] * pos_ref[...]
    cs_ref[0:ROPE_HALF, :] = jnp.cos(ang)
    cs_ref[ROPE_HALF:ROPE_DIM, :] = jnp.sin(ang)


def _rope_tables(positions):
    inv_freq = ROPE_THETA ** (-(jnp.arange(ROPE_HALF, dtype=jnp.float32) * 2.0 / ROPE_DIM))
    pos = positions.astype(jnp.float32).reshape(1, -1)
    return pl.pallas_call(
        _rope_kernel,
        out_shape=jax.ShapeDtypeStruct((ROPE_DIM, pos.shape[1]), jnp.float32),
        name="rope_tables",
    )(pos, inv_freq.reshape(ROPE_HALF, 1))


def _rope_expand_matrix():
    e = np.zeros((ROPE_DIM, 3 * LANES), np.float32)
    for lane in range(LANES):
        d = lane % HEAD_DIM
        if d < ROPE_HALF:
            e[d, lane] = 1.0
            e[ROPE_HALF + d, LANES + lane] = -1.0
        elif d < ROPE_DIM:
            e[d - ROPE_HALF, lane] = 1.0
            e[d, 2 * LANES + lane] = 1.0
    return e


def _down_groups(nch):
    cuts = list(range(0, nch - TAIL_GROUP + 1, DOWN_GROUP)) + [nch]
    groups = list(zip(cuts[:-1], cuts[1:]))
    assert groups[-1][1] - groups[-1][0] == TAIL_GROUP
    return groups


def _ffn_kernel(x_ref, gpre_ref, gpost_ref, wup_ref, cw_ref, wdown_ref, o_ref,
                ubuf, carry, h_ref, act_ref, f_ref, *, tm):
    i = pl.program_id(1)
    nk = FF_CHUNK // LANES
    nslab_half = D_FF // LANES
    nch = D_FF // FF_CHUNK
    groups = _down_groups(nch)
    group_of = [g for g, (c0, c1) in enumerate(groups) for _ in range(c0, c1)]

    @pl.when(i == 0)
    def _():
        carry[...] = jnp.zeros_like(carry)

    def up(c, ub, r0=0, rows=tm):
        sg = c * nk
        sv = nslab_half + c * nk
        if r0 == 0:
            ub[0:nk, :, 0:SUBLANES, :] = carry[sg:sg + nk]
            ub[nk:2 * nk, :, 0:SUBLANES, :] = carry[sv:sv + nk]
        h = h_ref[r0:r0 + rows, :]
        ug = jnp.dot(h, wup_ref[:, c * FF_CHUNK:(c + 1) * FF_CHUNK],
                     preferred_element_type=jnp.float32)
        uv = jnp.dot(h, wup_ref[:, D_FF + c * FF_CHUNK:D_FF + (c + 1) * FF_CHUNK],
                     preferred_element_type=jnp.float32)
        _to_slabs(ub.at[0:nk], SUBLANES + r0, ug)
        _to_slabs(ub.at[nk:2 * nk], SUBLANES + r0, uv)
        if r0 + rows == tm:
            carry[sg:sg + nk] = ub[0:nk, :, tm:tm + SUBLANES, :]
            carry[sv:sv + nk] = ub[nk:2 * nk, :, tm:tm + SUBLANES, :]

    def act(c, ub, r0=0, rows=tm):
        w = jnp.concatenate([cw_ref[:, c * nk:(c + 1) * nk],
                             cw_ref[:, nslab_half + c * nk:nslab_half + (c + 1) * nk]], axis=1)
        for r in range(r0, r0 + rows, ROW_BLK):
            y = _conv3(ub, r, ROW_BLK, w)
            yg = y[0:nk].astype(jnp.bfloat16)
            a = yg * jax.nn.sigmoid(yg) * y[nk:2 * nk].astype(jnp.bfloat16)
            g = group_of[c]
            k = c - groups[g][0]
            act_ref[g % 2, r:r + ROW_BLK, k * FF_CHUNK:(k + 1) * FF_CHUNK] = _from_slabs(a)

    def down(g, r0=0, rows=tm):
        c0, c1 = groups[g]
        part = jnp.dot(act_ref[g % 2, r0:r0 + rows, 0:(c1 - c0) * FF_CHUNK],
                       wdown_ref[c0 * FF_CHUNK:c1 * FF_CHUNK, :],
                       preferred_element_type=jnp.float32)
        if c0 == 0:
            f_ref[r0:r0 + rows, :] = part
        else:
            f_ref[r0:r0 + rows, :] += part

    nbuf = FFN_SKEW + 1
    bufs = [ubuf.at[k] for k in range(nbuf)]
    for r0 in range(0, tm, EDGE_BLK):
        _prenorm_rows(x_ref, gpre_ref, h_ref, r0, EDGE_BLK)
        up(0, bufs[0], r0, EDGE_BLK)
    for c in range(1, FFN_SKEW):
        up(c, bufs[c])
    last = nch - 1
    for c in range(last):
        if c + FFN_SKEW < nch:
            up(c + FFN_SKEW, bufs[(c + FFN_SKEW) % nbuf])
        act(c, bufs[c % nbuf])
        g = group_of[c]
        if c + 1 == groups[g][1]:
            down(g)
    for r0 in range(0, tm, TAIL_BLK):
        act(last, bufs[last % nbuf], r0, TAIL_BLK)
        down(len(groups) - 1, r0, TAIL_BLK)
        _residual_rows(x_ref, f_ref, gpost_ref, o_ref, r0, TAIL_BLK)


def _ffn(x, layer, g_pre, g_post, w_up, conv_w, w_down):
    b, s, d = x.shape
    tm = FFN_TILE
    nslab = 2 * D_FF // LANES
    nk = FF_CHUNK // LANES
    cw = conv_w.reshape(3, nslab, 1, 1, LANES)
    resident = (4 * tm * d * 4 + 2 * d * D_FF * 2 + D_FF * d * 2
                + 2 * tm * max(DOWN_GROUP, TAIL_GROUP) * FF_CHUNK * 2 + tm * d * 6
                + (FFN_SKEW + 1) * 2 * nk * (tm + SUBLANES) * LANES * 4)
    return pl.pallas_call(
        functools.partial(_ffn_kernel, tm=tm),
        grid=(b, s // tm),
        in_specs=[
            _x_spec(tm, d),
            _const_spec((1, d)),
            _const_spec((1, d)),
            _layer_spec(w_up.shape, layer),
            _const_spec(cw.shape),
            _layer_spec(w_down.shape, layer),
        ],
        out_specs=_x_spec(tm, d),
        out_shape=jax.ShapeDtypeStruct(x.shape, x.dtype),
        scratch_shapes=[
            pltpu.VMEM((FFN_SKEW + 1, 2 * nk, 1, tm + SUBLANES, LANES), jnp.float32),
            pltpu.VMEM((nslab, 1, SUBLANES, LANES), jnp.float32),
            pltpu.VMEM((tm, d), jnp.bfloat16),
            pltpu.VMEM((2, tm, max(DOWN_GROUP, TAIL_GROUP) * FF_CHUNK), jnp.bfloat16),
            pltpu.VMEM((tm, d), jnp.float32),
        ],
        compiler_params=pltpu.CompilerParams(
            dimension_semantics=("arbitrary", "arbitrary"),
            vmem_limit_bytes=_vmem_limit(resident)),
        name="ffn",
    )(x, g_pre.reshape(1, d), g_post.reshape(1, d), w_up, cw, w_down)


def _odd_kernel(x_ref, gpre_ref, gpost_ref, win_ref, cw_ref, wout_ref, o_ref,
                ubuf, carry, h_ref, y_ref, f_ref, *, tm):
    i = pl.program_id(1)
    nk = FF_CHUNK // LANES

    @pl.when(i == 0)
    def _():
        carry[...] = jnp.zeros_like(carry)

    nch = SC_DIM // FF_CHUNK

    def up(c, ub, r0=0, rows=tm):
        col = c * FF_CHUNK
        s0 = c * nk
        if r0 == 0:
            ub[0:nk, :, 0:SUBLANES, :] = carry[s0:s0 + nk]
        h = h_ref[r0:r0 + rows, :]
        zb = jnp.dot(h, win_ref[:, col:col + FF_CHUNK], preferred_element_type=jnp.float32)
        zc = jnp.dot(h, win_ref[:, SC_DIM + col:SC_DIM + col + FF_CHUNK],
                     preferred_element_type=jnp.float32)
        zu = jnp.dot(h, win_ref[:, 2 * SC_DIM + col:2 * SC_DIM + col + FF_CHUNK],
                     preferred_element_type=jnp.float32)
        _to_slabs(ub.at[0:nk], SUBLANES + r0, zc * zu)
        _to_slabs(ub.at[nk:2 * nk], SUBLANES + r0, zb)
        if r0 + rows == tm:
            carry[s0:s0 + nk] = ub[0:nk, :, tm:tm + SUBLANES, :]

    def act(c, ub, r0=0, rows=tm):
        w = cw_ref[:, c * nk:(c + 1) * nk]
        for r in range(r0, r0 + rows, ROW_BLK):
            y = _conv3(ub.at[0:nk], r, ROW_BLK, w)
            b_gate = ub[nk:2 * nk, :, SUBLANES + r:SUBLANES + r + ROW_BLK, :]
            y_ref[r:r + ROW_BLK, c * FF_CHUNK:(c + 1) * FF_CHUNK] = _from_slabs(
                (b_gate * y).astype(jnp.bfloat16))

    _prenorm_rows(x_ref, gpre_ref, h_ref, 0, tm)
    for c in range(nch):
        ub = ubuf.at[c % 2]
        up(c, ub)
        act(c, ub)
    f_ref[...] = jnp.dot(y_ref[...], wout_ref[...], preferred_element_type=jnp.float32)
    _residual_rows(x_ref, f_ref, gpost_ref, o_ref, 0, tm)


def _odd_mixer(x, g_pre, g_post, w_in, conv_w, w_out):
    b, s, d = x.shape
    tm = SEQ_TILE
    nslab = SC_DIM // LANES
    nk = FF_CHUNK // LANES
    cw = conv_w.reshape(3, nslab, 1, 1, LANES)
    resident = (4 * tm * d * 4 + w_in.size * 2 + w_out.size * 2 + tm * SC_DIM * 2
                + tm * d * 6 + 4 * nk * (tm + SUBLANES) * LANES * 4)
    return pl.pallas_call(
        functools.partial(_odd_kernel, tm=tm),
        grid=(b, s // tm),
        in_specs=[
            _x_spec(tm, d),
            _const_spec((1, d)),
            _const_spec((1, d)),
            _const_spec(w_in.shape),
            _const_spec(cw.shape),
            _const_spec(w_out.shape),
        ],
        out_specs=_x_spec(tm, d),
        out_shape=jax.ShapeDtypeStruct(x.shape, x.dtype),
        scratch_shapes=[
            pltpu.VMEM((2, 2 * nk, 1, tm + SUBLANES, LANES), jnp.float32),
            pltpu.VMEM((nslab, 1, SUBLANES, LANES), jnp.float32),
            pltpu.VMEM((tm, d), jnp.bfloat16),
            pltpu.VMEM((tm, SC_DIM), jnp.bfloat16),
            pltpu.VMEM((tm, d), jnp.float32),
        ],
        compiler_params=pltpu.CompilerParams(
            dimension_semantics=("arbitrary", "arbitrary"),
            vmem_limit_bytes=_vmem_limit(resident)),
        name="odd_mixer",
    )(x, g_pre.reshape(1, d), g_post.reshape(1, d), w_in, cw, w_out)


def _split3_bf16(x):
    hi = x.astype(jnp.bfloat16)
    r = x - hi.astype(jnp.float32)
    mid = r.astype(jnp.bfloat16)
    lo = (r - mid.astype(jnp.float32)).astype(jnp.bfloat16)
    return hi, mid, lo


def _rope(x, c, s_up, s_dn):
    up = pltpu.roll(x, LANES - ROPE_HALF, axis=1)
    dn = pltpu.roll(x, ROPE_HALF, axis=1)
    return x * c + up * s_up + dn * s_dn


def _band_bias():
    r = np.arange(GROUP * WINDOW)[:, None] % WINDOW
    col = np.arange(2 * WINDOW)[None, :]
    band = (col > r) & (col <= r + WINDOW)
    first = band & (col >= WINDOW)
    return np.where(np.stack([band, first]), 0.0, -np.inf).astype(np.float32)


def _even_kernel(x_ref, cs_ref, e_ref, bias_ref, gpre_ref, gpost_ref, win_ref, cw_ref, cb_ref,
                 lng_ref, lnb_ref, sink_ref, wout_ref, o_ref,
                 abuf, h_ref, q_ref, kg_ref, vg_ref, mix_ref, f_ref, *, tm):
    i = pl.program_id(1)
    nslab = A_CH // LANES

    @pl.when(i == 0)
    def _():
        abuf[:, :, 0:A_HALO, :] = jnp.zeros((nslab, 1, A_HALO, LANES), jnp.float32)
        kg_ref[:, 0:WINDOW, :] = jnp.zeros((N_KV_HEADS, WINDOW, LANES), jnp.bfloat16)
        vg_ref[:, 0:WINDOW, 0:LANES] = jnp.zeros((N_KV_HEADS, WINDOW, LANES), jnp.bfloat16)
        vg_ref[:, :, LANES:2 * LANES] = jnp.ones((N_KV_HEADS, tm + WINDOW, LANES), jnp.bfloat16)

    @pl.when(i > 0)
    def _():
        abuf[:, :, 0:A_HALO, :] = abuf[:, :, tm:tm + A_HALO, :]
        kg_ref[:, 0:WINDOW, :] = kg_ref[:, tm:tm + WINDOW, :]
        vg_ref[:, 0:WINDOW, 0:LANES] = vg_ref[:, tm:tm + WINDOW, 0:LANES]

    def attend(j, g):
        qs = jnp.concatenate(
            [q_ref[j * WINDOW:(j + 1) * WINDOW, p * LANES:(p + 1) * LANES] for p in range(GROUP)],
            axis=0)
        kw = kg_ref[g, j * WINDOW:(j + 2) * WINDOW, :]
        vw = vg_ref[g, j * WINDOW:(j + 2) * WINDOW, :]
        sc = lax.dot_general(qs, kw, (((1,), (1,)), ((), ())),
                             preferred_element_type=jnp.float32)
        sc = sc + (bias_ref[jnp.where(i > 0, 0, 1)] if j == 0 else bias_ref[0])
        sink = sink_ref[g]
        m = jnp.maximum(jnp.max(sc, axis=-1, keepdims=True), sink)
        p_un = jnp.exp2(sc - jnp.concatenate([m, m], axis=-1))
        pv = jnp.dot(p_un.astype(jnp.bfloat16), vw, preferred_element_type=jnp.float32)
        denom = pv[:, LANES:2 * LANES] + jnp.exp2(sink - m)
        return pv[:, 0:LANES] * (1.0 / denom)

    def conv_ln(r0):
        acc = jnp.broadcast_to(cb_ref[...], (nslab, 1, CONV_ROWS, LANES))
        for tap in range(A_CONV):
            acc = acc + _rows(abuf, r0 + (A_HALO - A_CONV + 1) + tap, CONV_ROWS) * cw_ref[tap]
        y = _from_slabs(acc)
        mu = jnp.mean(y, axis=-1, keepdims=True)
        yc = y - mu
        y = yc * lax.rsqrt(jnp.mean(yc * yc, axis=-1, keepdims=True) + LN_EPS)
        y = (y * lng_ref[...] + lnb_ref[...]).astype(jnp.bfloat16)
        mix_ref[r0:r0 + CONV_ROWS, 0:A_CH] = y * jax.nn.sigmoid(y)

    rows = WINDOW

    def glu_in(j):
        r0 = j * rows
        _prenorm_rows(x_ref, gpre_ref, h_ref, r0, rows)
        za = jnp.dot(h_ref[r0:r0 + rows, :], win_ref[:, 0:2 * A_CH],
                     preferred_element_type=jnp.float32)
        _to_slabs(abuf, A_HALO + r0, za[:, 0:A_CH] * jax.nn.sigmoid(za[:, A_CH:2 * A_CH]))

    def qkv_in(j):
        r0 = j * rows
        t = None
        for piece in _split3_bf16(cs_ref[:, r0:r0 + rows]):
            d = lax.dot_general(piece, e_ref[...], (((0,), (0,)), ((), ())),
                                preferred_element_type=jnp.float32)
            t = d if t is None else t + d
        lane = lax.broadcasted_iota(jnp.int32, (rows, LANES), 1)
        c = t[:, 0:LANES] + jnp.where((lane & (HEAD_DIM - 1)) >= ROPE_DIM, 1.0, 0.0)
        s_up = t[:, LANES:2 * LANES]
        s_dn = t[:, 2 * LANES:3 * LANES]

        zq = jnp.dot(h_ref[r0:r0 + rows, :], win_ref[:, 2 * A_CH:EVEN_IN],
                     preferred_element_type=jnp.float32)
        cq, squ, sqd = c * LOG2E, s_up * LOG2E, s_dn * LOG2E
        for p in range(GROUP):
            q_ref[r0:r0 + rows, p * LANES:(p + 1) * LANES] = _rope(
                zq[:, p * LANES:(p + 1) * LANES], cq, squ, sqd).astype(jnp.bfloat16)
        k = _rope(zq[:, Q_DIM:Q_DIM + KV_DIM], c, s_up, s_dn).astype(jnp.bfloat16)
        v = zq[:, Q_DIM + KV_DIM:Q_DIM + 2 * KV_DIM].astype(jnp.bfloat16)
        zero = jnp.zeros_like(k)
        for g in range(N_KV_HEADS):
            mine = (lane >= g * HEAD_DIM) & (lane < (g + 1) * HEAD_DIM)
            kg_ref[g, WINDOW + r0:WINDOW + r0 + rows, :] = jnp.where(mine, k, zero)
            vg_ref[g, WINDOW + r0:WINDOW + r0 + rows, 0:LANES] = jnp.where(mine, v, zero)

    def out(j, nb):
        r0 = j * rows
        f_ref[r0:r0 + nb * rows, :] = jnp.dot(mix_ref[r0:r0 + nb * rows, :], wout_ref[...],
                                              preferred_element_type=jnp.float32)
        _residual_rows(x_ref, f_ref, gpost_ref, o_ref, r0, nb * rows)

    nblk = tm // rows
    conv_per_blk = rows // CONV_ROWS
    for j in range(min(2, nblk)):
        glu_in(j)
        qkv_in(j)
    for j in range(nblk):
        o = attend(j, 0)
        if j + 2 < nblk:
            glu_in(j + 2)
        for cc in range(conv_per_blk // 2):
            conv_ln(j * rows + cc * CONV_ROWS)
        o = o + attend(j, 1)
        if j + 2 < nblk:
            qkv_in(j + 2)
        for cc in range(conv_per_blk // 2, conv_per_blk):
            conv_ln(j * rows + cc * CONV_ROWS)
        mix_ref[j * rows:(j + 1) * rows, A_CH:2 * A_CH] = jnp.concatenate(
            [o[p * WINDOW:(p + 1) * WINDOW] for p in range(GROUP)], axis=-1).astype(jnp.bfloat16)
        if j % OUT_BLKS == OUT_BLKS - 1:
            out(j + 1 - OUT_BLKS, OUT_BLKS)
    assert nblk % OUT_BLKS == 0


def _pair_perm():
    idx = []
    for p in range(GROUP):
        for half in range(N_KV_HEADS):
            head = p + GROUP * half
            idx.extend(range(head * HEAD_DIM, (head + 1) * HEAD_DIM))
    return np.array(idx, dtype=np.int32)


def _even_mixer(x, cs, g_pre, g_post, w_in, conv_w, conv_b, ln_g, ln_b, sinks, w_out):
    b, s, d = x.shape
    tm = EVEN_TILE
    nslab = A_CH // LANES
    perm = _pair_perm()
    wq = w_in[:, 2 * A_CH:2 * A_CH + Q_DIM][:, perm] * (HEAD_DIM ** -0.5)
    w_in_p = jnp.concatenate([w_in[:, :2 * A_CH], wq, w_in[:, 2 * A_CH + Q_DIM:]],
                             axis=1).astype(jnp.bfloat16)
    w_out_p = jnp.concatenate([w_out[:A_CH], w_out[A_CH:][perm]], axis=0).astype(jnp.bfloat16)
    cw = conv_w.reshape(A_CONV, nslab, 1, 1, LANES)
    cb = conv_b.reshape(nslab, 1, 1, LANES)
    sink_rows = jnp.broadcast_to(
        jnp.repeat(sinks.reshape(N_KV_HEADS, GROUP) * LOG2E, WINDOW, axis=1)[..., None],
        (N_KV_HEADS, GROUP * WINDOW, LANES))
    e = jnp.asarray(_rope_expand_matrix(), jnp.bfloat16)
    bias = jnp.asarray(_band_bias())
    steps = s // tm
    cs_spec = pl.BlockSpec((ROPE_DIM, tm), lambda bb, i: (0, bb * steps + i))
    resident = (4 * tm * d * 4 + 2 * ROPE_DIM * tm * 4 + w_in_p.size * 2 + w_out_p.size * 2
                + bias.size * 4 + nslab * (tm + A_HALO) * LANES * 4 + tm * d * 6 + tm * Q_DIM * 2
                + N_KV_HEADS * (tm + WINDOW) * 3 * LANES * 2 + tm * d * 2)
    return pl.pallas_call(
        functools.partial(_even_kernel, tm=tm),
        grid=(b, steps),
        in_specs=[
            _x_spec(tm, d), cs_spec,
            _const_spec(e.shape),
            _const_spec(bias.shape),
            _const_spec((1, d)),
            _const_spec((1, d)),
            _const_spec(w_in_p.shape),
            _const_spec(cw.shape),
            _const_spec(cb.shape),
            _const_spec((1, A_CH)),
            _const_spec((1, A_CH)),
            _const_spec(sink_rows.shape),
            _const_spec(w_out_p.shape),
        ],
        out_specs=_x_spec(tm, d),
        out_shape=jax.ShapeDtypeStruct(x.shape, x.dtype),
        scratch_shapes=[
            pltpu.VMEM((nslab, 1, tm + A_HALO, LANES), jnp.float32),
            pltpu.VMEM((tm, d), jnp.bfloat16),
            pltpu.VMEM((tm, Q_DIM), jnp.bfloat16),
            pltpu.VMEM((N_KV_HEADS, tm + WINDOW, LANES), jnp.bfloat16),
            pltpu.VMEM((N_KV_HEADS, tm + WINDOW, 2 * LANES), jnp.bfloat16),
            pltpu.VMEM((tm, 2 * A_CH), jnp.bfloat16),
            pltpu.VMEM((tm, d), jnp.float32),
        ],
        compiler_params=pltpu.CompilerParams(
            dimension_semantics=("arbitrary", "arbitrary"),
            vmem_limit_bytes=_vmem_limit(resident)),
        name="even_mixer",
    )(x, cs, e, bias, g_pre.reshape(1, d), g_post.reshape(1, d), w_in_p, cw, cb,
      ln_g.reshape(1, A_CH), ln_b.reshape(1, A_CH), sink_rows, w_out_p)


def kernel(x, positions, mix_norm_pre, mix_norm_post, ffn_norm_pre, ffn_norm_post, ev_w_in, ev_a_conv_w, ev_a_conv_b, ev_a_ln_g, ev_a_ln_b, ev_sinks, ev_w_out, od_w_in, od_conv_w, od_w_out, ffn_w_up, ffn_conv_w, ffn_w_down):
    depth = mix_norm_pre.shape[0]
    d = x.shape[2]
    assert all(x.shape[1] % t == 0 for t in (SEQ_TILE, EVEN_TILE, FFN_TILE))
    assert EVEN_TILE % WINDOW == 0
    cs = _rope_tables(positions)
    bf = jnp.bfloat16
    w_up = ffn_w_up.astype(bf)
    w_down = ffn_w_down.astype(bf)
    for i in range(depth):
        j = i // 2
        if i % 2 == 0:
            x = _even_mixer(x, cs, mix_norm_pre[i], mix_norm_post[i], ev_w_in[j],
                            ev_a_conv_w[j], ev_a_conv_b[j], ev_a_ln_g[j], ev_a_ln_b[j],
                            ev_sinks[j], ev_w_out[j])
        else:
            x = _odd_mixer(x, mix_norm_pre[i], mix_norm_post[i], od_w_in[j].astype(bf),
                           od_conv_w[j], od_w_out[j].astype(bf))
        x = _ffn(x, i, ffn_norm_pre[i], ffn_norm_post[i], w_up, ffn_conv_w[i], w_down)
    return x
```

```python
import functools

import jax
import jax.numpy as jnp
import numpy as np
from jax import lax
from jax.experimental import pallas as pl
from jax.experimental.pallas import tpu as pltpu

D_MODEL = 1024
HEAD_DIM = 64
A_CH = 512
A_CONV = 31
N_Q_HEADS = 8
N_KV_HEADS = 2
GROUP = 4
WINDOW = 128
ROPE_THETA = 500000.0
ROPE_DIM = 16
ROPE_HALF = ROPE_DIM // 2
Q_DIM = 512
KV_DIM = 128
EVEN_IN = 1792
SC_DIM = 1024
D_FF = 2816
RMS_EPS = 1e-6
LN_EPS = 1e-5
LOG2E = 1.4426950408889634

LANES = 128
SUBLANES = 8
VMEM_BYTES = 64 << 20

SEQ_TILE = 1024
FFN_TILE = 512
EVEN_TILE = 1024
FF_CHUNK = 256
CONV_ROWS = 32
OUT_BLKS = 2
ROW_BLK = 64
EDGE_BLK = 256
FFN_SKEW = 3
DOWN_GROUP = 2
TAIL_GROUP = 1
TAIL_BLK = 256
A_HALO = 32


def _rows(ref, start, size):
    return ref[:, pl.ds(0, 1, stride=2), pl.ds(start, size), :]


def _rms(x, g):
    ms = jnp.mean(x * x, axis=-1, keepdims=True)
    return x * lax.rsqrt(ms + RMS_EPS) * g


def _to_slabs(ref, row0, val):
    rows = val.shape[0]
    for k in range(val.shape[1] // LANES):
        ref[k, 0, row0:row0 + rows, :] = val[:, k * LANES:(k + 1) * LANES]


def _from_slabs(val):
    return jnp.concatenate([val[k, 0] for k in range(val.shape[0])], axis=-1)


def _conv3(ubuf, r0, rows, w):
    u0 = ubuf[:, :, SUBLANES + r0:SUBLANES + r0 + rows, :]
    u1 = _rows(ubuf, SUBLANES - 1 + r0, rows)
    u2 = _rows(ubuf, SUBLANES - 2 + r0, rows)
    return w[2] * u0 + w[1] * u1 + w[0] * u2


def _prenorm_rows(x_ref, g_ref, h_ref, r0, rows):
    g = g_ref[...]
    for r in range(r0, r0 + rows, ROW_BLK):
        h_ref[r:r + ROW_BLK, :] = _rms(x_ref[r:r + ROW_BLK, :], g).astype(jnp.bfloat16)


def _residual_rows(x_ref, f_ref, g_ref, o_ref, r0, rows):
    g = g_ref[...]
    for r in range(r0, r0 + rows, ROW_BLK):
        o_ref[r:r + ROW_BLK, :] = x_ref[r:r + ROW_BLK, :] + _rms(f_ref[r:r + ROW_BLK, :], g)


def _const_spec(shape):
    nd = len(shape)
    return pl.BlockSpec(shape, lambda b, i: (0,) * nd, pipeline_mode=pl.Buffered(1))


def _layer_spec(shape, layer):
    nd = len(shape)
    return pl.BlockSpec((None,) + tuple(shape[1:]), lambda b, i: (layer,) + (0,) * (nd - 1),
                        pipeline_mode=pl.Buffered(1))


def _x_spec(tm, width):
    return pl.BlockSpec((None, tm, width), lambda b, i: (b, i, 0))


def _vmem_limit(resident_bytes):
    return int(min(VMEM_BYTES - (4 << 20), 2 * resident_bytes + (8 << 20)))


def _rope_kernel(pos_ref, invf_ref, cs_ref):
    ang = invf_ref[...] * pos_ref[...]
    cs_ref[0:ROPE_HALF, :] = jnp.cos(ang)
    cs_ref[ROPE_HALF:ROPE_DIM, :] = jnp.sin(ang)


def _rope_tables(positions):
    inv_freq = ROPE_THETA ** (-(jnp.arange(ROPE_HALF, dtype=jnp.float32) * 2.0 / ROPE_DIM))
    pos = positions.astype(jnp.float32).reshape(1, -1)
    return pl.pallas_call(
        _rope_kernel,
        out_shape=jax.ShapeDtypeStruct((ROPE_DIM, pos.shape[1]), jnp.float32),
        name="rope_tables",
    )(pos, inv_freq.reshape(ROPE_HALF, 1))


def _rope_expand_matrix():
    e = np.zeros((ROPE_DIM, 3 * LANES), np.float32)
    for lane in range(LANES):
        d = lane % HEAD_DIM
        if d < ROPE_HALF:
            e[d, lane] = 1.0
            e[ROPE_HALF + d, LANES + lane] = -1.0
        elif d < ROPE_DIM:
            e[d - ROPE_HALF, lane] = 1.0
            e[d, 2 * LANES + lane] = 1.0
    return e


def _down_groups(nch):
    cuts = list(range(0, nch - TAIL_GROUP + 1, DOWN_GROUP)) + [nch]
    groups = list(zip(cuts[:-1], cuts[1:]))
    assert groups[-1][1] - groups[-1][0] == TAIL_GROUP
    return groups


def _ffn_kernel(x_ref, gpre_ref, gpost_ref, wup_ref, cw_ref, wdown_ref, o_ref,
                ubuf, carry, h_ref, act_ref, f_ref, *, tm):
    i = pl.program_id(1)
    nk = FF_CHUNK // LANES
    nslab_half = D_FF // LANES
    nch = D_FF // FF_CHUNK
    groups = _down_groups(nch)
    group_of = [g for g, (c0, c1) in enumerate(groups) for _ in range(c0, c1)]

    @pl.when(i == 0)
    def _():
        carry[...] = jnp.zeros_like(carry)

    def up(c, ub, r0=0, rows=tm):
        sg = c * nk
        sv = nslab_half + c * nk
        if r0 == 0:
            ub[0:nk, :, 0:SUBLANES, :] = carry[sg:sg + nk]
            ub[nk:2 * nk, :, 0:SUBLANES, :] = carry[sv:sv + nk]
        h = h_ref[r0:r0 + rows, :]
        ug = jnp.dot(h, wup_ref[:, c * FF_CHUNK:(c + 1) * FF_CHUNK],
                     preferred_element_type=jnp.float32)
        uv = jnp.dot(h, wup_ref[:, D_FF + c * FF_CHUNK:D_FF + (c + 1) * FF_CHUNK],
                     preferred_element_type=jnp.float32)
        _to_slabs(ub.at[0:nk], SUBLANES + r0, ug)
        _to_slabs(ub.at[nk:2 * nk], SUBLANES + r0, uv)
        if r0 + rows == tm:
            carry[sg:sg + nk] = ub[0:nk, :, tm:tm + SUBLANES, :]
            carry[sv:sv + nk] = ub[nk:2 * nk, :, tm:tm + SUBLANES, :]

    def act(c, ub, r0=0, rows=tm):
        w = jnp.concatenate([cw_ref[:, c * nk:(c + 1) * nk],
                             cw_ref[:, nslab_half + c * nk:nslab_half + (c + 1) * nk]], axis=1)
        for r in range(r0, r0 + rows, ROW_BLK):
            y = _conv3(ub, r, ROW_BLK, w)
            yg = y[0:nk].astype(jnp.bfloat16)
            a = yg * jax.nn.sigmoid(yg) * y[nk:2 * nk].astype(jnp.bfloat16)
            g = group_of[c]
            k = c - groups[g][0]
            act_ref[g % 2, r:r + ROW_BLK, k * FF_CHUNK:(k + 1) * FF_CHUNK] = _from_slabs(a)

    def down(g, r0=0, rows=tm):
        c0, c1 = groups[g]
        part = jnp.dot(act_ref[g % 2, r0:r0 + rows, 0:(c1 - c0) * FF_CHUNK],
                       wdown_ref[c0 * FF_CHUNK:c1 * FF_CHUNK, :],
                       preferred_element_type=jnp.float32)
        if c0 == 0:
            f_ref[r0:r0 + rows, :] = part
        else:
            f_ref[r0:r0 + rows, :] += part

    nbuf = FFN_SKEW + 1
    bufs = [ubuf.at[k] for k in range(nbuf)]
    for r0 in range(0, tm, EDGE_BLK):
        _prenorm_rows(x_ref, gpre_ref, h_ref, r0, EDGE_BLK)
        up(0, bufs[0], r0, EDGE_BLK)
    for c in range(1, FFN_SKEW):
        up(c, bufs[c])
    last = nch - 1
    for c in range(last):
        if c + FFN_SKEW < nch:
            up(c + FFN_SKEW, bufs[(c + FFN_SKEW) % nbuf])
        act(c, bufs[c % nbuf])
        g = group_of[c]
        if c + 1 == groups[g][1]:
            down(g)
    for r0 in range(0, tm, TAIL_BLK):
        act(last, bufs[last % nbuf], r0, TAIL_BLK)
        down(len(groups) - 1, r0, TAIL_BLK)
        _residual_rows(x_ref, f_ref, gpost_ref, o_ref, r0, TAIL_BLK)


def _ffn(x, layer, g_pre, g_post, w_up, conv_w, w_down):
    b, s, d = x.shape
    tm = FFN_TILE
    nslab = 2 * D_FF // LANES
    nk = FF_CHUNK // LANES
    cw = conv_w.reshape(3, nslab, 1, 1, LANES)
    resident = (4 * tm * d * 4 + 2 * d * D_FF * 2 + D_FF * d * 2
                + 2 * tm * max(DOWN_GROUP, TAIL_GROUP) * FF_CHUNK * 2 + tm * d * 6
                + (FFN_SKEW + 1) * 2 * nk * (tm + SUBLANES) * LANES * 4)
    return pl.pallas_call(
        functools.partial(_ffn_kernel, tm=tm),
        grid=(b, s // tm),
        in_specs=[
            _x_spec(tm, d),
            _const_spec((1, d)),
            _const_spec((1, d)),
            _layer_spec(w_up.shape, layer),
            _const_spec(cw.shape),
            _layer_spec(w_down.shape, layer),
        ],
        out_specs=_x_spec(tm, d),
        out_shape=jax.ShapeDtypeStruct(x.shape, x.dtype),
        scratch_shapes=[
            pltpu.VMEM((FFN_SKEW + 1, 2 * nk, 1, tm + SUBLANES, LANES), jnp.float32),
            pltpu.VMEM((nslab, 1, SUBLANES, LANES), jnp.float32),
            pltpu.VMEM((tm, d), jnp.bfloat16),
            pltpu.VMEM((2, tm, max(DOWN_GROUP, TAIL_GROUP) * FF_CHUNK), jnp.bfloat16),
            pltpu.VMEM((tm, d), jnp.float32),
        ],
        compiler_params=pltpu.CompilerParams(
            dimension_semantics=("arbitrary", "arbitrary"),
            vmem_limit_bytes=_vmem_limit(resident)),
        name="ffn",
    )(x, g_pre.reshape(1, d), g_post.reshape(1, d), w_up, cw, w_down)


def _odd_kernel(x_ref, gpre_ref, gpost_ref, win_ref, cw_ref, wout_ref, o_ref,
                ubuf, carry, h_ref, y_ref, f_ref, *, tm):
    i = pl.program_id(1)
    nk = FF_CHUNK // LANES

    @pl.when(i == 0)
    def _():
        carry[...] = jnp.zeros_like(carry)

    nch = SC_DIM // FF_CHUNK

    def up(c, ub, r0=0, rows=tm):
        col = c * FF_CHUNK
        s0 = c * nk
        if r0 == 0:
            ub[0:nk, :, 0:SUBLANES, :] = carry[s0:s0 + nk]
        h = h_ref[r0:r0 + rows, :]
        zb = jnp.dot(h, win_ref[:, col:col + FF_CHUNK], preferred_element_type=jnp.float32)
        zc = jnp.dot(h, win_ref[:, SC_DIM + col:SC_DIM + col + FF_CHUNK],
                     preferred_element_type=jnp.float32)
        zu = jnp.dot(h, win_ref[:, 2 * SC_DIM + col:2 * SC_DIM + col + FF_CHUNK],
                     preferred_element_type=jnp.float32)
        _to_slabs(ub.at[0:nk], SUBLANES + r0, zc * zu)
        _to_slabs(ub.at[nk:2 * nk], SUBLANES + r0, zb)
        if r0 + rows == tm:
            carry[s0:s0 + nk] = ub[0:nk, :, tm:tm + SUBLANES, :]

    def act(c, ub, r0=0, rows=tm):
        w = cw_ref[:, c * nk:(c + 1) * nk]
        for r in range(r0, r0 + rows, ROW_BLK):
            y = _conv3(ub.at[0:nk], r, ROW_BLK, w)
            b_gate = ub[nk:2 * nk, :, SUBLANES + r:SUBLANES + r + ROW_BLK, :]
            y_ref[r:r + ROW_BLK, c * FF_CHUNK:(c + 1) * FF_CHUNK] = _from_slabs(
                (b_gate * y).astype(jnp.bfloat16))

    _prenorm_rows(x_ref, gpre_ref, h_ref, 0, tm)
    for c in range(nch):
        ub = ubuf.at[c % 2]
        up(c, ub)
        act(c, ub)
    f_ref[...] = jnp.dot(y_ref[...], wout_ref[...], preferred_element_type=jnp.float32)
    _residual_rows(x_ref, f_ref, gpost_ref, o_ref, 0, tm)


def _odd_mixer(x, g_pre, g_post, w_in, conv_w, w_out):
    b, s, d = x.shape
    tm = SEQ_TILE
    nslab = SC_DIM // LANES
    nk = FF_CHUNK // LANES
    cw = conv_w.reshape(3, nslab, 1, 1, LANES)
    resident = (4 * tm * d * 4 + w_in.size * 2 + w_out.size * 2 + tm * SC_DIM * 2
                + tm * d * 6 + 4 * nk * (tm + SUBLANES) * LANES * 4)
    return pl.pallas_call(
        functools.partial(_odd_kernel, tm=tm),
        grid=(b, s // tm),
        in_specs=[
            _x_spec(tm, d),
            _const_spec((1, d)),
            _const_spec((1, d)),
            _const_spec(w_in.shape),
            _const_spec(cw.shape),
            _const_spec(w_out.shape),
        ],
        out_specs=_x_spec(tm, d),
        out_shape=jax.ShapeDtypeStruct(x.shape, x.dtype),
        scratch_shapes=[
            pltpu.VMEM((2, 2 * nk, 1, tm + SUBLANES, LANES), jnp.float32),
            pltpu.VMEM((nslab, 1, SUBLANES, LANES), jnp.float32),
            pltpu.VMEM((tm, d), jnp.bfloat16),
            pltpu.VMEM((tm, SC_DIM), jnp.bfloat16),
            pltpu.VMEM((tm, d), jnp.float32),
        ],
        compiler_params=pltpu.CompilerParams(
            dimension_semantics=("arbitrary", "arbitrary"),
            vmem_limit_bytes=_vmem_limit(resident)),
        name="odd_mixer",
    )(x, g_pre.reshape(1, d), g_post.reshape(1, d), w_in, cw, w_out)


def _split3_bf16(x):
    hi = x.astype(jnp.bfloat16)
    r = x - hi.astype(jnp.float32)
    mid = r.astype(jnp.bfloat16)
    lo = (r - mid.astype(jnp.float32)).astype(jnp.bfloat16)
    return hi, mid, lo


def _rope(x, c, s_up, s_dn):
    up = pltpu.roll(x, LANES - ROPE_HALF, axis=1)
    dn = pltpu.roll(x, ROPE_HALF, axis=1)
    return x * c + up * s_up + dn * s_dn


def _band_bias():
    r = np.arange(GROUP * WINDOW)[:, None] % WINDOW
    col = np.arange(2 * WINDOW)[None, :]
    band = (col > r) & (col <= r + WINDOW)
    first = band & (col >= WINDOW)
    return np.where(np.stack([band, first]), 0.0, -np.inf).astype(np.float32)


def _even_kernel(x_ref, cs_ref, e_ref, bias_ref, gpre_ref, gpost_ref, win_ref, cw_ref, cb_ref,
                 lng_ref, lnb_ref, sink_ref, wout_ref, o_ref,
                 abuf, h_ref, q_ref, kg_ref, vg_ref, mix_ref, f_ref, *, tm):
    i = pl.program_id(1)
    nslab = A_CH // LANES

    @pl.when(i == 0)
    def _():
        abuf[:, :, 0:A_HALO, :] = jnp.zeros((nslab, 1, A_HALO, LANES), jnp.float32)
        kg_ref[:, 0:WINDOW, :] = jnp.zeros((N_KV_HEADS, WINDOW, LANES), jnp.bfloat16)
        vg_ref[:, 0:WINDOW, 0:LANES] = jnp.zeros((N_KV_HEADS, WINDOW, LANES), jnp.bfloat16)
        vg_ref[:, :, LANES:2 * LANES] = jnp.ones((N_KV_HEADS, tm + WINDOW, LANES), jnp.bfloat16)

    @pl.when(i > 0)
    def _():
        abuf[:, :, 0:A_HALO, :] = abuf[:, :, tm:tm + A_HALO, :]
        kg_ref[:, 0:WINDOW, :] = kg_ref[:, tm:tm + WINDOW, :]
        vg_ref[:, 0:WINDOW, 0:LANES] = vg_ref[:, tm:tm + WINDOW, 0:LANES]

    def attend(j, g):
        qs = jnp.concatenate(
            [q_ref[j * WINDOW:(j + 1) * WINDOW, p * LANES:(p + 1) * LANES] for p in range(GROUP)],
            axis=0)
        kw = kg_ref[g, j * WINDOW:(j + 2) * WINDOW, :]
        vw = vg_ref[g, j * WINDOW:(j + 2) * WINDOW, :]
        sc = lax.dot_general(qs, kw, (((1,), (1,)), ((), ())),
                             preferred_element_type=jnp.float32)
        sc = sc + (bias_ref[jnp.where(i > 0, 0, 1)] if j == 0 else bias_ref[0])
        sink = sink_ref[g]
        m = jnp.maximum(jnp.max(sc, axis=-1, keepdims=True), sink)
        p_un = jnp.exp2(sc - jnp.concatenate([m, m], axis=-1))
        pv = jnp.dot(p_un.astype(jnp.bfloat16), vw, preferred_element_type=jnp.float32)
        denom = pv[:, LANES:2 * LANES] + jnp.exp2(sink - m)
        return pv[:, 0:LANES] * (1.0 / denom)

    def conv_ln(r0):
        acc = jnp.broadcast_to(cb_ref[...], (nslab, 1, CONV_ROWS, LANES))
        for tap in range(A_CONV):
            acc = acc + _rows(abuf, r0 + (A_HALO - A_CONV + 1) + tap, CONV_ROWS) * cw_ref[tap]
        y = _from_slabs(acc)
        mu = jnp.mean(y, axis=-1, keepdims=True)
        yc = y - mu
        y = yc * lax.rsqrt(jnp.mean(yc * yc, axis=-1, keepdims=True) + LN_EPS)
        y = y * lng_ref[...] + lnb_ref[...]
        mix_ref[r0:r0 + CONV_ROWS, 0:A_CH] = (y * jax.nn.sigmoid(y)).astype(jnp.bfloat16)

    rows = WINDOW

    def glu_in(j):
        r0 = j * rows
        _prenorm_rows(x_ref, gpre_ref, h_ref, r0, rows)
        za = jnp.dot(h_ref[r0:r0 + rows, :], win_ref[:, 0:2 * A_CH],
                     preferred_element_type=jnp.float32)
        _to_slabs(abuf, A_HALO + r0, za[:, 0:A_CH] * jax.nn.sigmoid(za[:, A_CH:2 * A_CH]))

    def qkv_in(j):
        r0 = j * rows
        t = None
        for piece in _split3_bf16(cs_ref[:, r0:r0 + rows]):
            d = lax.dot_general(piece, e_ref[...], (((0,), (0,)), ((), ())),
                                preferred_element_type=jnp.float32)
            t = d if t is None else t + d
        lane = lax.broadcasted_iota(jnp.int32, (rows, LANES), 1)
        c = t[:, 0:LANES] + jnp.where((lane & (HEAD_DIM - 1)) >= ROPE_DIM, 1.0, 0.0)
        s_up = t[:, LANES:2 * LANES]
        s_dn = t[:, 2 * LANES:3 * LANES]

        zq = jnp.dot(h_ref[r0:r0 + rows, :], win_ref[:, 2 * A_CH:EVEN_IN],
                     preferred_element_type=jnp.float32)
        cq, squ, sqd = c * LOG2E, s_up * LOG2E, s_dn * LOG2E
        for p in range(GROUP):
            q_ref[r0:r0 + rows, p * LANES:(p + 1) * LANES] = _rope(
                zq[:, p * LANES:(p + 1) * LANES], cq, squ, sqd).astype(jnp.bfloat16)
        k = _rope(zq[:, Q_DIM:Q_DIM + KV_DIM], c, s_up, s_dn).astype(jnp.bfloat16)
        v = zq[:, Q_DIM + KV_DIM:Q_DIM + 2 * KV_DIM].astype(jnp.bfloat16)
        zero = jnp.zeros_like(k)
        for g in range(N_KV_HEADS):
            mine = (lane >= g * HEAD_DIM) & (lane < (g + 1) * HEAD_DIM)
            kg_ref[g, WINDOW + r0:WINDOW + r0 + rows, :] = jnp.where(mine, k, zero)
            vg_ref[g, WINDOW + r0:WINDOW + r0 + rows, 0:LANES] = jnp.where(mine, v, zero)

    def out(j, nb):
        r0 = j * rows
        f_ref[r0:r0 + nb * rows, :] = jnp.dot(mix_ref[r0:r0 + nb * rows, :], wout_ref[...],
                                              preferred_element_type=jnp.float32)
        _residual_rows(x_ref, f_ref, gpost_ref, o_ref, r0, nb * rows)

    nblk = tm // rows
    conv_per_blk = rows // CONV_ROWS
    for j in range(min(2, nblk)):
        glu_in(j)
        qkv_in(j)
    for j in range(nblk):
        o = attend(j, 0)
        if j + 2 < nblk:
            glu_in(j + 2)
        for cc in range(conv_per_blk // 2):
            conv_ln(j * rows + cc * CONV_ROWS)
        o = o + attend(j, 1)
        if j + 2 < nblk:
            qkv_in(j + 2)
        for cc in range(conv_per_blk // 2, conv_per_blk):
            conv_ln(j * rows + cc * CONV_ROWS)
        mix_ref[j * rows:(j + 1) * rows, A_CH:2 * A_CH] = jnp.concatenate(
            [o[p * WINDOW:(p + 1) * WINDOW] for p in range(GROUP)], axis=-1).astype(jnp.bfloat16)
        if j % OUT_BLKS == OUT_BLKS - 1:
            out(j + 1 - OUT_BLKS, OUT_BLKS)
    assert nblk % OUT_BLKS == 0


def _pair_perm():
    idx = []
    for p in range(GROUP):
        for half in range(N_KV_HEADS):
            head = p + GROUP * half
            idx.extend(range(head * HEAD_DIM, (head + 1) * HEAD_DIM))
    return np.array(idx, dtype=np.int32)


def _even_mixer(x, cs, g_pre, g_post, w_in, conv_w, conv_b, ln_g, ln_b, sinks, w_out):
    b, s, d = x.shape
    tm = EVEN_TILE
    nslab = A_CH // LANES
    perm = _pair_perm()
    wq = w_in[:, 2 * A_CH:2 * A_CH + Q_DIM][:, perm] * (HEAD_DIM ** -0.5)
    w_in_p = jnp.concatenate([w_in[:, :2 * A_CH], wq, w_in[:, 2 * A_CH + Q_DIM:]],
                             axis=1).astype(jnp.bfloat16)
    w_out_p = jnp.concatenate([w_out[:A_CH], w_out[A_CH:][perm]], axis=0).astype(jnp.bfloat16)
    cw = conv_w.reshape(A_CONV, nslab, 1, 1, LANES)
    cb = conv_b.reshape(nslab, 1, 1, LANES)
    sink_rows = jnp.broadcast_to(
        jnp.repeat(sinks.reshape(N_KV_HEADS, GROUP) * LOG2E, WINDOW, axis=1)[..., None],
        (N_KV_HEADS, GROUP * WINDOW, LANES))
    e = jnp.asarray(_rope_expand_matrix(), jnp.bfloat16)
    bias = jnp.asarray(_band_bias())
    steps = s // tm
    cs_spec = pl.BlockSpec((ROPE_DIM, tm), lambda bb, i: (0, bb * steps + i))
    resident = (4 * tm * d * 4 + 2 * ROPE_DIM * tm * 4 + w_in_p.size * 2 + w_out_p.size * 2
                + bias.size * 4 + nslab * (tm + A_HALO) * LANES * 4 + tm * d * 6 + tm * Q_DIM * 2
                + N_KV_HEADS * (tm + WINDOW) * 3 * LANES * 2 + tm * d * 2)
    return pl.pallas_call(
        functools.partial(_even_kernel, tm=tm),
        grid=(b, steps),
        in_specs=[
            _x_spec(tm, d), cs_spec,
            _const_spec(e.shape),
            _const_spec(bias.shape),
            _const_spec((1, d)),
            _const_spec((1, d)),
            _const_spec(w_in_p.shape),
            _const_spec(cw.shape),
            _const_spec(cb.shape),
            _const_spec((1, A_CH)),
            _const_spec((1, A_CH)),
            _const_spec(sink_rows.shape),
            _const_spec(w_out_p.shape),
        ],
        out_specs=_x_spec(tm, d),
        out_shape=jax.ShapeDtypeStruct(x.shape, x.dtype),
        scratch_shapes=[
            pltpu.VMEM((nslab, 1, tm + A_HALO, LANES), jnp.float32),
            pltpu.VMEM((tm, d), jnp.bfloat16),
            pltpu.VMEM((tm, Q_DIM), jnp.bfloat16),
            pltpu.VMEM((N_KV_HEADS, tm + WINDOW, LANES), jnp.bfloat16),
            pltpu.VMEM((N_KV_HEADS, tm + WINDOW, 2 * LANES), jnp.bfloat16),
            pltpu.VMEM((tm, 2 * A_CH), jnp.bfloat16),
            pltpu.VMEM((tm, d), jnp.float32),
        ],
        compiler_params=pltpu.CompilerParams(
            dimension_semantics=("arbitrary", "arbitrary"),
            vmem_limit_bytes=_vmem_limit(resident)),
        name="even_mixer",
    )(x, cs, e, bias, g_pre.reshape(1, d), g_post.reshape(1, d), w_in_p, cw, cb,
      ln_g.reshape(1, A_CH), ln_b.reshape(1, A_CH), sink_rows, w_out_p)


def kernel(x, positions, mix_norm_pre, mix_norm_post, ffn_norm_pre, ffn_norm_post, ev_w_in, ev_a_conv_w, ev_a_conv_b, ev_a_ln_g, ev_a_ln_b, ev_sinks, ev_w_out, od_w_in, od_conv_w, od_w_out, ffn_w_up, ffn_conv_w, ffn_w_down):
    depth = mix_norm_pre.shape[0]
    d = x.shape[2]
    assert all(x.shape[1] % t == 0 for t in (SEQ_TILE, EVEN_TILE, FFN_TILE))
    assert EVEN_TILE % WINDOW == 0
    cs = _rope_tables(positions)
    bf = jnp.bfloat16
    w_up = ffn_w_up.astype(bf)
    w_down = ffn_w_down.astype(bf)
    for i in range(depth):
        j = i // 2
        if i % 2 == 0:
            x = _even_mixer(x, cs, mix_norm_pre[i], mix_norm_post[i], ev_w_in[j],
                            ev_a_conv_w[j], ev_a_conv_b[j], ev_a_ln_g[j], ev_a_ln_b[j],
                            ev_sinks[j], ev_w_out[j])
        else:
            x = _odd_mixer(x, mix_norm_pre[i], mix_norm_post[i], od_w_in[j].astype(bf),
                           od_conv_w[j], od_w_out[j].astype(bf))
        x = _ffn(x, i, ffn_norm_pre[i], ffn_norm_post[i], w_up, ffn_conv_w[i], w_down)
    return x
```

```python
import functools

import jax
import jax.numpy as jnp
import numpy as np
from jax import lax
from jax.experimental import pallas as pl
from jax.experimental.pallas import tpu as pltpu

D_MODEL = 1024
HEAD_DIM = 64
A_CH = 512
A_CONV = 31
N_Q_HEADS = 8
N_KV_HEADS = 2
GROUP = 4
WINDOW = 128
ROPE_THETA = 500000.0
ROPE_DIM = 16
ROPE_HALF = ROPE_DIM // 2
Q_DIM = 512
KV_DIM = 128
EVEN_IN = 1792
SC_DIM = 1024
D_FF = 2816
RMS_EPS = 1e-6
LN_EPS = 1e-5
LOG2E = 1.4426950408889634

LANES = 128
SUBLANES = 8
VMEM_BYTES = 64 << 20

SEQ_TILE = 1024
FFN_TILE = 512
EVEN_TILE = 1024
FF_CHUNK = 256
CONV_ROWS = 32
OUT_BLKS = 2
ROW_BLK = 64
EDGE_BLK = 256
FFN_SKEW = 3
DOWN_GROUP = 2
TAIL_GROUP = 1
TAIL_BLK = 256
A_HALO = 32


def _rows(ref, start, size):
    return ref[:, pl.ds(0, 1, stride=2), pl.ds(start, size), :]


def _rms(x, g):
    ms = jnp.mean(x * x, axis=-1, keepdims=True)
    return x * lax.rsqrt(ms + RMS_EPS) * g


def _to_slabs(ref, row0, val):
    rows = val.shape[0]
    for k in range(val.shape[1] // LANES):
        ref[k, 0, row0:row0 + rows, :] = val[:, k * LANES:(k + 1) * LANES]


def _from_slabs(val):
    return jnp.concatenate([val[k, 0] for k in range(val.shape[0])], axis=-1)


def _conv3(ubuf, r0, rows, w):
    u0 = ubuf[:, :, SUBLANES + r0:SUBLANES + r0 + rows, :]
    u1 = _rows(ubuf, SUBLANES - 1 + r0, rows)
    u2 = _rows(ubuf, SUBLANES - 2 + r0, rows)
    return w[2] * u0 + w[1] * u1 + w[0] * u2


def _prenorm_rows(x_ref, h_ref, r0, rows):
    for r in range(r0, r0 + rows, ROW_BLK):
        x = x_ref[r:r + ROW_BLK, :]
        ms = jnp.mean(x * x, axis=-1, keepdims=True)
        h_ref[r:r + ROW_BLK, :] = (x * lax.rsqrt(ms + RMS_EPS)).astype(jnp.bfloat16)


def _residual_rows(x_ref, f_ref, g_ref, o_ref, r0, rows):
    g = g_ref[...]
    for r in range(r0, r0 + rows, ROW_BLK):
        o_ref[r:r + ROW_BLK, :] = x_ref[r:r + ROW_BLK, :] + _rms(f_ref[r:r + ROW_BLK, :], g)


def _const_spec(shape):
    nd = len(shape)
    return pl.BlockSpec(shape, lambda b, i: (0,) * nd, pipeline_mode=pl.Buffered(1))


def _layer_spec(shape, layer):
    nd = len(shape)
    return pl.BlockSpec((None,) + tuple(shape[1:]), lambda b, i: (layer,) + (0,) * (nd - 1),
                        pipeline_mode=pl.Buffered(1))


def _x_spec(tm, width):
    return pl.BlockSpec((None, tm, width), lambda b, i: (b, i, 0))


def _vmem_limit(resident_bytes):
    return int(min(VMEM_BYTES - (4 << 20), 2 * resident_bytes + (8 << 20)))


def _rope_kernel(pos_ref, invf_ref, cs_ref):
    ang = invf_ref[...] * pos_ref[...]
    cs_ref[0:ROPE_HALF, :] = jnp.cos(ang)
    cs_ref[ROPE_HALF:ROPE_DIM, :] = jnp.sin(ang)


def _rope_tables(positions):
    inv_freq = ROPE_THETA ** (-(jnp.arange(ROPE_HALF, dtype=jnp.float32) * 2.0 / ROPE_DIM))
    pos = positions.astype(jnp.float32).reshape(1, -1)
    return pl.pallas_call(
        _rope_kernel,
        out_shape=jax.ShapeDtypeStruct((ROPE_DIM, pos.shape[1]), jnp.float32),
        name="rope_tables",
    )(pos, inv_freq.reshape(ROPE_HALF, 1))


def _rope_expand_matrix():
    e = np.zeros((ROPE_DIM, 3 * LANES), np.float32)
    for lane in range(LANES):
        d = lane % HEAD_DIM
        if d < ROPE_HALF:
            e[d, lane] = 1.0
            e[ROPE_HALF + d, LANES + lane] = -1.0
        elif d < ROPE_DIM:
            e[d - ROPE_HALF, lane] = 1.0
            e[d, 2 * LANES + lane] = 1.0
    return e


def _down_groups(nch):
    cuts = list(range(0, nch - TAIL_GROUP + 1, DOWN_GROUP)) + [nch]
    groups = list(zip(cuts[:-1], cuts[1:]))
    assert groups[-1][1] - groups[-1][0] == TAIL_GROUP
    return groups


def _ffn_kernel(x_ref, gpost_ref,wup_ref, cw_ref, wdown_ref, o_ref,
                ubuf, carry, h_ref, act_ref, f_ref, *, tm):
    i = pl.program_id(1)
    nk = FF_CHUNK // LANES
    nslab_half = D_FF // LANES
    nch = D_FF // FF_CHUNK
    groups = _down_groups(nch)
    group_of = [g for g, (c0, c1) in enumerate(groups) for _ in range(c0, c1)]

    @pl.when(i == 0)
    def _():
        carry[...] = jnp.zeros_like(carry)

    def up(c, ub, r0=0, rows=tm):
        sg = c * nk
        sv = nslab_half + c * nk
        if r0 == 0:
            ub[0:nk, :, 0:SUBLANES, :] = carry[sg:sg + nk]
            ub[nk:2 * nk, :, 0:SUBLANES, :] = carry[sv:sv + nk]
        h = h_ref[r0:r0 + rows, :]
        ug = jnp.dot(h, wup_ref[:, c * FF_CHUNK:(c + 1) * FF_CHUNK],
                     preferred_element_type=jnp.float32)
        uv = jnp.dot(h, wup_ref[:, D_FF + c * FF_CHUNK:D_FF + (c + 1) * FF_CHUNK],
                     preferred_element_type=jnp.float32)
        _to_slabs(ub.at[0:nk], SUBLANES + r0, ug)
        _to_slabs(ub.at[nk:2 * nk], SUBLANES + r0, uv)
        if r0 + rows == tm:
            carry[sg:sg + nk] = ub[0:nk, :, tm:tm + SUBLANES, :]
            carry[sv:sv + nk] = ub[nk:2 * nk, :, tm:tm + SUBLANES, :]

    def act(c, ub, r0=0, rows=tm):
        w = jnp.concatenate([cw_ref[:, c * nk:(c + 1) * nk],
                             cw_ref[:, nslab_half + c * nk:nslab_half + (c + 1) * nk]], axis=1)
        for r in range(r0, r0 + rows, ROW_BLK):
            y = _conv3(ub, r, ROW_BLK, w)
            yg = y[0:nk].astype(jnp.bfloat16)
            a = yg * jax.nn.sigmoid(yg) * y[nk:2 * nk].astype(jnp.bfloat16)
            g = group_of[c]
            k = c - groups[g][0]
            act_ref[g % 2, r:r + ROW_BLK, k * FF_CHUNK:(k + 1) * FF_CHUNK] = _from_slabs(a)

    def down(g, r0=0, rows=tm):
        c0, c1 = groups[g]
        part = jnp.dot(act_ref[g % 2, r0:r0 + rows, 0:(c1 - c0) * FF_CHUNK],
                       wdown_ref[c0 * FF_CHUNK:c1 * FF_CHUNK, :],
                       preferred_element_type=jnp.float32)
        if c0 == 0:
            f_ref[r0:r0 + rows, :] = part
        else:
            f_ref[r0:r0 + rows, :] += part

    nbuf = FFN_SKEW + 1
    bufs = [ubuf.at[k] for k in range(nbuf)]
    for r0 in range(0, tm, EDGE_BLK):
        _prenorm_rows(x_ref, h_ref,r0, EDGE_BLK)
        up(0, bufs[0], r0, EDGE_BLK)
    for c in range(1, FFN_SKEW):
        up(c, bufs[c])
    last = nch - 1
    for c in range(last):
        if c + FFN_SKEW < nch:
            up(c + FFN_SKEW, bufs[(c + FFN_SKEW) % nbuf])
        act(c, bufs[c % nbuf])
        g = group_of[c]
        if c + 1 == groups[g][1]:
            down(g)
    for r0 in range(0, tm, TAIL_BLK):
        act(last, bufs[last % nbuf], r0, TAIL_BLK)
        down(len(groups) - 1, r0, TAIL_BLK)
        _residual_rows(x_ref, f_ref, gpost_ref, o_ref, r0, TAIL_BLK)


def _ffn(x, layer, g_post, w_up, conv_w, w_down):
    b, s, d = x.shape
    tm = FFN_TILE
    nslab = 2 * D_FF // LANES
    nk = FF_CHUNK // LANES
    cw = conv_w.reshape(3, nslab, 1, 1, LANES)
    resident = (4 * tm * d * 4 + 2 * d * D_FF * 2 + D_FF * d * 2
                + 2 * tm * max(DOWN_GROUP, TAIL_GROUP) * FF_CHUNK * 2 + tm * d * 6
                + (FFN_SKEW + 1) * 2 * nk * (tm + SUBLANES) * LANES * 4)
    return pl.pallas_call(
        functools.partial(_ffn_kernel, tm=tm),
        grid=(b, s // tm),
        in_specs=[
            _x_spec(tm, d),
            _const_spec((1, d)),
            _layer_spec(w_up.shape, layer),
            _const_spec(cw.shape),
            _layer_spec(w_down.shape, layer),
        ],
        out_specs=_x_spec(tm, d),
        out_shape=jax.ShapeDtypeStruct(x.shape, x.dtype),
        scratch_shapes=[
            pltpu.VMEM((FFN_SKEW + 1, 2 * nk, 1, tm + SUBLANES, LANES), jnp.float32),
            pltpu.VMEM((nslab, 1, SUBLANES, LANES), jnp.float32),
            pltpu.VMEM((tm, d), jnp.bfloat16),
            pltpu.VMEM((2, tm, max(DOWN_GROUP, TAIL_GROUP) * FF_CHUNK), jnp.bfloat16),
            pltpu.VMEM((tm, d), jnp.float32),
        ],
        compiler_params=pltpu.CompilerParams(
            dimension_semantics=("arbitrary", "arbitrary"),
            vmem_limit_bytes=_vmem_limit(resident)),
        name="ffn",
    )(x, g_post.reshape(1, d),w_up, cw, w_down)


def _odd_kernel(x_ref, gpost_ref,win_ref, cw_ref, wout_ref, o_ref,
                ubuf, carry, h_ref, y_ref, f_ref, *, tm):
    i = pl.program_id(1)
    nk = FF_CHUNK // LANES

    @pl.when(i == 0)
    def _():
        carry[...] = jnp.zeros_like(carry)

    nch = SC_DIM // FF_CHUNK

    def up(c, ub, r0=0, rows=tm):
        col = c * FF_CHUNK
        s0 = c * nk
        if r0 == 0:
            ub[0:nk, :, 0:SUBLANES, :] = carry[s0:s0 + nk]
        h = h_ref[r0:r0 + rows, :]
        zb = jnp.dot(h, win_ref[:, col:col + FF_CHUNK], preferred_element_type=jnp.float32)
        zc = jnp.dot(h, win_ref[:, SC_DIM + col:SC_DIM + col + FF_CHUNK],
                     preferred_element_type=jnp.float32)
        zu = jnp.dot(h, win_ref[:, 2 * SC_DIM + col:2 * SC_DIM + col + FF_CHUNK],
                     preferred_element_type=jnp.float32)
        _to_slabs(ub.at[0:nk], SUBLANES + r0, zc * zu)
        _to_slabs(ub.at[nk:2 * nk], SUBLANES + r0, zb)
        if r0 + rows == tm:
            carry[s0:s0 + nk] = ub[0:nk, :, tm:tm + SUBLANES, :]

    def act(c, ub, r0=0, rows=tm):
        w = cw_ref[:, c * nk:(c + 1) * nk]
        for r in range(r0, r0 + rows, ROW_BLK):
            y = _conv3(ub.at[0:nk], r, ROW_BLK, w)
            b_gate = ub[nk:2 * nk, :, SUBLANES + r:SUBLANES + r + ROW_BLK, :]
            y_ref[r:r + ROW_BLK, c * FF_CHUNK:(c + 1) * FF_CHUNK] = _from_slabs(
                (b_gate * y).astype(jnp.bfloat16))

    _prenorm_rows(x_ref, h_ref,0, tm)
    for c in range(nch):
        ub = ubuf.at[c % 2]
        up(c, ub)
        act(c, ub)
    f_ref[...] = jnp.dot(y_ref[...], wout_ref[...], preferred_element_type=jnp.float32)
    _residual_rows(x_ref, f_ref, gpost_ref, o_ref, 0, tm)


def _odd_mixer(x, g_post, w_in, conv_w, w_out):
    b, s, d = x.shape
    tm = SEQ_TILE
    nslab = SC_DIM // LANES
    nk = FF_CHUNK // LANES
    cw = conv_w.reshape(3, nslab, 1, 1, LANES)
    resident = (4 * tm * d * 4 + w_in.size * 2 + w_out.size * 2 + tm * SC_DIM * 2
                + tm * d * 6 + 4 * nk * (tm + SUBLANES) * LANES * 4)
    return pl.pallas_call(
        functools.partial(_odd_kernel, tm=tm),
        grid=(b, s // tm),
        in_specs=[
            _x_spec(tm, d),
            _const_spec((1, d)),
            _const_spec(w_in.shape),
            _const_spec(cw.shape),
            _const_spec(w_out.shape),
        ],
        out_specs=_x_spec(tm, d),
        out_shape=jax.ShapeDtypeStruct(x.shape, x.dtype),
        scratch_shapes=[
            pltpu.VMEM((2, 2 * nk, 1, tm + SUBLANES, LANES), jnp.float32),
            pltpu.VMEM((nslab, 1, SUBLANES, LANES), jnp.float32),
            pltpu.VMEM((tm, d), jnp.bfloat16),
            pltpu.VMEM((tm, SC_DIM), jnp.bfloat16),
            pltpu.VMEM((tm, d), jnp.float32),
        ],
        compiler_params=pltpu.CompilerParams(
            dimension_semantics=("arbitrary", "arbitrary"),
            vmem_limit_bytes=_vmem_limit(resident)),
        name="odd_mixer",
    )(x, g_post.reshape(1, d),w_in, cw, w_out)


def _split3_bf16(x):
    hi = x.astype(jnp.bfloat16)
    r = x - hi.astype(jnp.float32)
    mid = r.astype(jnp.bfloat16)
    lo = (r - mid.astype(jnp.float32)).astype(jnp.bfloat16)
    return hi, mid, lo


def _rope(x, c, s_up, s_dn):
    up = pltpu.roll(x, LANES - ROPE_HALF, axis=1)
    dn = pltpu.roll(x, ROPE_HALF, axis=1)
    return x * c + up * s_up + dn * s_dn


def _band_bias():
    r = np.arange(GROUP * WINDOW)[:, None] % WINDOW
    col = np.arange(2 * WINDOW)[None, :]
    band = (col > r) & (col <= r + WINDOW)
    first = band & (col >= WINDOW)
    return np.where(np.stack([band, first]), 0.0, -np.inf).astype(np.float32)


def _even_kernel(x_ref, cs_ref, e_ref, bias_ref, gpost_ref,win_ref, cw_ref, cb_ref,
                 lng_ref, lnb_ref, sink_ref, wout_ref, o_ref,
                 abuf, h_ref, q_ref, kg_ref, vg_ref, mix_ref, f_ref, *, tm):
    i = pl.program_id(1)
    nslab = A_CH // LANES

    @pl.when(i == 0)
    def _():
        abuf[:, :, 0:A_HALO, :] = jnp.zeros((nslab, 1, A_HALO, LANES), jnp.float32)
        kg_ref[:, 0:WINDOW, :] = jnp.zeros((N_KV_HEADS, WINDOW, LANES), jnp.bfloat16)
        vg_ref[:, 0:WINDOW, 0:LANES] = jnp.zeros((N_KV_HEADS, WINDOW, LANES), jnp.bfloat16)
        vg_ref[:, :, LANES:2 * LANES] = jnp.ones((N_KV_HEADS, tm + WINDOW, LANES), jnp.bfloat16)

    @pl.when(i > 0)
    def _():
        abuf[:, :, 0:A_HALO, :] = abuf[:, :, tm:tm + A_HALO, :]
        kg_ref[:, 0:WINDOW, :] = kg_ref[:, tm:tm + WINDOW, :]
        vg_ref[:, 0:WINDOW, 0:LANES] = vg_ref[:, tm:tm + WINDOW, 0:LANES]

    def attend(j, g):
        qs = jnp.concatenate(
            [q_ref[j * WINDOW:(j + 1) * WINDOW, p * LANES:(p + 1) * LANES] for p in range(GROUP)],
            axis=0)
        kw = kg_ref[g, j * WINDOW:(j + 2) * WINDOW, :]
        vw = vg_ref[g, j * WINDOW:(j + 2) * WINDOW, :]
        sc = lax.dot_general(qs, kw, (((1,), (1,)), ((), ())),
                             preferred_element_type=jnp.float32)
        sc = sc + (bias_ref[jnp.where(i > 0, 0, 1)] if j == 0 else bias_ref[0])
        sink = sink_ref[g]
        m = jnp.maximum(jnp.max(sc, axis=-1, keepdims=True), sink)
        p_un = jnp.exp2(sc - jnp.concatenate([m, m], axis=-1))
        pv = jnp.dot(p_un.astype(jnp.bfloat16), vw, preferred_element_type=jnp.float32)
        denom = pv[:, LANES:2 * LANES] + jnp.exp2(sink - m)
        return pv[:, 0:LANES] * (1.0 / denom)

    def conv_ln(r0):
        acc = jnp.broadcast_to(cb_ref[...], (nslab, 1, CONV_ROWS, LANES))
        for tap in range(A_CONV):
            acc = acc + _rows(abuf, r0 + (A_HALO - A_CONV + 1) + tap, CONV_ROWS) * cw_ref[tap]
        y = _from_slabs(acc)
        mu = jnp.mean(y, axis=-1, keepdims=True)
        yc = y - mu
        y = yc * lax.rsqrt(jnp.mean(yc * yc, axis=-1, keepdims=True) + LN_EPS)
        y = y * lng_ref[...] + lnb_ref[...]
        mix_ref[r0:r0 + CONV_ROWS, 0:A_CH] = (y * jax.nn.sigmoid(y)).astype(jnp.bfloat16)

    rows = WINDOW

    def glu_in(j):
        r0 = j * rows
        _prenorm_rows(x_ref, h_ref,r0, rows)
        za = jnp.dot(h_ref[r0:r0 + rows, :], win_ref[:, 0:2 * A_CH],
                     preferred_element_type=jnp.float32)
        _to_slabs(abuf, A_HALO + r0, za[:, 0:A_CH] * jax.nn.sigmoid(za[:, A_CH:2 * A_CH]))

    def qkv_in(j):
        r0 = j * rows
        t = None
        for piece in _split3_bf16(cs_ref[:, r0:r0 + rows]):
            d = lax.dot_general(piece, e_ref[...], (((0,), (0,)), ((), ())),
                                preferred_element_type=jnp.float32)
            t = d if t is None else t + d
        lane = lax.broadcasted_iota(jnp.int32, (rows, LANES), 1)
        c = t[:, 0:LANES] + jnp.where((lane & (HEAD_DIM - 1)) >= ROPE_DIM, 1.0, 0.0)
        s_up = t[:, LANES:2 * LANES]
        s_dn = t[:, 2 * LANES:3 * LANES]

        zq = jnp.dot(h_ref[r0:r0 + rows, :], win_ref[:, 2 * A_CH:EVEN_IN],
                     preferred_element_type=jnp.float32)
        cq, squ, sqd = c * LOG2E, s_up * LOG2E, s_dn * LOG2E
        for p in range(GROUP):
            q_ref[r0:r0 + rows, p * LANES:(p + 1) * LANES] = _rope(
                zq[:, p * LANES:(p + 1) * LANES], cq, squ, sqd).astype(jnp.bfloat16)
        k = _rope(zq[:, Q_DIM:Q_DIM + KV_DIM], c, s_up, s_dn).astype(jnp.bfloat16)
        v = zq[:, Q_DIM + KV_DIM:Q_DIM + 2 * KV_DIM].astype(jnp.bfloat16)
        zero = jnp.zeros_like(k)
        for g in range(N_KV_HEADS):
            mine = (lane >= g * HEAD_DIM) & (lane < (g + 1) * HEAD_DIM)
            kg_ref[g, WINDOW + r0:WINDOW + r0 + rows, :] = jnp.where(mine, k, zero)
            vg_ref[g, WINDOW + r0:WINDOW + r0 + rows, 0:LANES] = jnp.where(mine, v, zero)

    def out(j, nb):
        r0 = j * rows
        f_ref[r0:r0 + nb * rows, :] = jnp.dot(mix_ref[r0:r0 + nb * rows, :], wout_ref[...],
                                              preferred_element_type=jnp.float32)
        _residual_rows(x_ref, f_ref, gpost_ref, o_ref, r0, nb * rows)

    nblk = tm // rows
    conv_per_blk = rows // CONV_ROWS
    for j in range(min(2, nblk)):
        glu_in(j)
        qkv_in(j)
    for j in range(nblk):
        o = attend(j, 0)
        if j + 2 < nblk:
            glu_in(j + 2)
        for cc in range(conv_per_blk // 2):
            conv_ln(j * rows + cc * CONV_ROWS)
        o = o + attend(j, 1)
        if j + 2 < nblk:
            qkv_in(j + 2)
        for cc in range(conv_per_blk // 2, conv_per_blk):
            conv_ln(j * rows + cc * CONV_ROWS)
        mix_ref[j * rows:(j + 1) * rows, A_CH:2 * A_CH] = jnp.concatenate(
            [o[p * WINDOW:(p + 1) * WINDOW] for p in range(GROUP)], axis=-1).astype(jnp.bfloat16)
        if j % OUT_BLKS == OUT_BLKS - 1:
            out(j + 1 - OUT_BLKS, OUT_BLKS)
    assert nblk % OUT_BLKS == 0


def _pair_perm():
    idx = []
    for p in range(GROUP):
        for half in range(N_KV_HEADS):
            head = p + GROUP * half
            idx.extend(range(head * HEAD_DIM, (head + 1) * HEAD_DIM))
    return np.array(idx, dtype=np.int32)


def _even_mixer(x, cs, g_post, w_in, conv_w, conv_b, ln_g, ln_b, sinks, w_out):
    b, s, d = x.shape
    tm = EVEN_TILE
    nslab = A_CH // LANES
    perm = _pair_perm()
    wq = w_in[:, 2 * A_CH:2 * A_CH + Q_DIM][:, perm] * (HEAD_DIM ** -0.5)
    w_in_p = jnp.concatenate([w_in[:, :2 * A_CH], wq, w_in[:, 2 * A_CH + Q_DIM:]],
                             axis=1).astype(jnp.bfloat16)
    w_out_p = jnp.concatenate([w_out[:A_CH], w_out[A_CH:][perm]], axis=0).astype(jnp.bfloat16)
    cw = conv_w.reshape(A_CONV, nslab, 1, 1, LANES)
    cb = conv_b.reshape(nslab, 1, 1, LANES)
    sink_rows = jnp.broadcast_to(
        jnp.repeat(sinks.reshape(N_KV_HEADS, GROUP) * LOG2E, WINDOW, axis=1)[..., None],
        (N_KV_HEADS, GROUP * WINDOW, LANES))
    e = jnp.asarray(_rope_expand_matrix(), jnp.bfloat16)
    bias = jnp.asarray(_band_bias())
    steps = s // tm
    cs_spec = pl.BlockSpec((ROPE_DIM, tm), lambda bb, i: (0, bb * steps + i))
    resident = (4 * tm * d * 4 + 2 * ROPE_DIM * tm * 4 + w_in_p.size * 2 + w_out_p.size * 2
                + bias.size * 4 + nslab * (tm + A_HALO) * LANES * 4 + tm * d * 6 + tm * Q_DIM * 2
                + N_KV_HEADS * (tm + WINDOW) * 3 * LANES * 2 + tm * d * 2)
    return pl.pallas_call(
        functools.partial(_even_kernel, tm=tm),
        grid=(b, steps),
        in_specs=[
            _x_spec(tm, d), cs_spec,
            _const_spec(e.shape),
            _const_spec(bias.shape),
            _const_spec((1, d)),
            _const_spec(w_in_p.shape),
            _const_spec(cw.shape),
            _const_spec(cb.shape),
            _const_spec((1, A_CH)),
            _const_spec((1, A_CH)),
            _const_spec(sink_rows.shape),
            _const_spec(w_out_p.shape),
        ],
        out_specs=_x_spec(tm, d),
        out_shape=jax.ShapeDtypeStruct(x.shape, x.dtype),
        scratch_shapes=[
            pltpu.VMEM((nslab, 1, tm + A_HALO, LANES), jnp.float32),
            pltpu.VMEM((tm, d), jnp.bfloat16),
            pltpu.VMEM((tm, Q_DIM), jnp.bfloat16),
            pltpu.VMEM((N_KV_HEADS, tm + WINDOW, LANES), jnp.bfloat16),
            pltpu.VMEM((N_KV_HEADS, tm + WINDOW, 2 * LANES), jnp.bfloat16),
            pltpu.VMEM((tm, 2 * A_CH), jnp.bfloat16),
            pltpu.VMEM((tm, d), jnp.float32),
        ],
        compiler_params=pltpu.CompilerParams(
            dimension_semantics=("arbitrary", "arbitrary"),
            vmem_limit_bytes=_vmem_limit(resident)),
        name="even_mixer",
    )(x, cs, e, bias, g_post.reshape(1, d),w_in_p, cw, cb,
      ln_g.reshape(1, A_CH), ln_b.reshape(1, A_CH), sink_rows, w_out_p)


def kernel(x, positions, mix_norm_pre, mix_norm_post, ffn_norm_pre, ffn_norm_post, ev_w_in, ev_a_conv_w, ev_a_conv_b, ev_a_ln_g, ev_a_ln_b, ev_sinks, ev_w_out, od_w_in, od_conv_w, od_w_out, ffn_w_up, ffn_conv_w, ffn_w_down):
    depth = mix_norm_pre.shape[0]
    d = x.shape[2]
    assert all(x.shape[1] % t == 0 for t in (SEQ_TILE, EVEN_TILE, FFN_TILE))
    assert EVEN_TILE % WINDOW == 0
    cs = _rope_tables(positions)
    bf = jnp.bfloat16
    w_up = (ffn_w_up * ffn_norm_pre[:, :, None]).astype(bf)
    w_down = ffn_w_down.astype(bf)
    for i in range(depth):
        j = i // 2
        if i % 2 == 0:
            x = _even_mixer(x, cs, mix_norm_post[i], ev_w_in[j] * mix_norm_pre[i][:, None],
                            ev_a_conv_w[j], ev_a_conv_b[j], ev_a_ln_g[j], ev_a_ln_b[j],
                            ev_sinks[j], ev_w_out[j])
        else:
            x = _odd_mixer(x, mix_norm_post[i],
                           (od_w_in[j] * mix_norm_pre[i][:, None]).astype(bf),
                           od_conv_w[j], od_w_out[j].astype(bf))
        x = _ffn(x, i, ffn_norm_post[i], w_up, ffn_conv_w[i], w_down)
    return x
```
